```python
import math
import jax, jax.numpy as jnp
from jax import lax
import numpy as np

D_MODEL = 1024
BATCH = 2
SEQ = 8192
DEPTH = 4
DEC_BATCH = 128
DEC_SEQ = 1
PAST_LEN = 8192
PAGE_SIZE = 128

HEAD_DIM = 64
MIX_WIDTH = D_MODEL
H_A = (MIX_WIDTH // 2) // HEAD_DIM
H_B = (MIX_WIDTH // 2) // HEAD_DIM
KV_B = 2
G_B = H_B // KV_B
DILATIONS = ((128, 1), (512, 4), (2048, 16))
WIN_A = max(w for w, _ in DILATIONS)
WIN_B = 128
BLOCK = 128
N_EXPERTS = 32
N_GROUPS = 8
EXPERTS_PER_GROUP = N_EXPERTS // N_GROUPS
TOP_K = 2
D_EXPERT = D_MODEL // 4
ALPHA = (2.0 * DEPTH) ** 0.25
BETA = (8.0 * DEPTH) ** -0.25
LN_EPS = 1e-5
NEG_INF = -1e30
ATTN_SCALE = HEAD_DIM ** -0.5
A_WIDTH = H_A * HEAD_DIM
B_Q = H_B * HEAD_DIM
B_KV = KV_B * HEAD_DIM
SPLIT_POINTS = (A_WIDTH, 2 * A_WIDTH, 3 * A_WIDTH, 3 * A_WIDTH + B_Q, 3 * A_WIDTH + B_Q + B_KV)
IN_COLS = 3 * A_WIDTH + B_Q + 2 * B_KV

kernel_name = "hymba_dilated_swa_sink_groupmoe_step"

F32 = jnp.float32


def alibi_slopes(n):
    return jnp.asarray(2.0 ** (-8.0 * np.arange(1, n + 1) / n), dtype=F32)


def layer_norm(x, g, b):
    xf = x.astype(F32)
    mu = jnp.mean(xf, -1, keepdims=True)
    var = jnp.mean(jnp.square(xf - mu), -1, keepdims=True)
    return ((xf - mu) * lax.rsqrt(var + LN_EPS) * g.astype(F32) + b.astype(F32)).astype(x.dtype)


def rms_norm(x, g):
    xf = x.astype(F32)
    return xf * lax.rsqrt(jnp.mean(jnp.square(xf), -1, keepdims=True) + LN_EPS) * g.astype(F32)


def masked_softmax(s, sink=None):
    m = jnp.max(s, axis=-1, keepdims=True)
    if sink is not None:
        m = jnp.maximum(m, sink)
    p = jnp.exp(s - m)
    den = jnp.sum(p, axis=-1, keepdims=True)
    if sink is not None:
        den = den + jnp.exp(sink - m)
    return p / den, (m + jnp.log(den))[..., 0]


def banded_attention(q, k, v, n_steps, dist_scale, slopes, sink):
    N, L, KV, G, Dh = q.shape
    nb = -(-L // BLOCK)
    pad = nb * BLOCK - L
    qb = jnp.pad(q, ((0, 0), (0, pad), (0, 0), (0, 0), (0, 0))).reshape(N, nb, BLOCK, KV, G, Dh)
    kp = jnp.pad(k, ((0, 0), (BLOCK, pad), (0, 0), (0, 0))).reshape(N, nb + 1, BLOCK, KV, Dh)
    vp = jnp.pad(v, ((0, 0), (BLOCK, pad), (0, 0), (0, 0))).reshape(N, nb + 1, BLOCK, KV, Dh)
    kw = jnp.concatenate([kp[:, :-1], kp[:, 1:]], axis=2)
    vw = jnp.concatenate([vp[:, :-1], vp[:, 1:]], axis=2)
    s = jnp.einsum('nbqhgd,nbkhd->nbhgqk', qb, kw, preferred_element_type=F32) * ATTN_SCALE
    dist = jnp.arange(BLOCK)[:, None] + BLOCK - jnp.arange(2 * BLOCK)[None, :]
    key_pos = jnp.arange(nb)[:, None] * BLOCK - BLOCK + jnp.arange(2 * BLOCK)[None, :]
    valid = ((dist >= 0) & (dist <= n_steps))[None] & (key_pos >= 0)[:, None, :]
    bias = -(slopes[:, :, None, None] * (dist * dist_scale).astype(F32))
    s = jnp.where(valid[None, :, None, None], s + bias, NEG_INF)
    sk = None if sink is None else sink.astype(F32)[None, None, :, :, None, None]
    p, lse = masked_softmax(s, sk)
    o = jnp.einsum('nbhgqk,nbkhd->nbqhgd', p, vw.astype(F32)).reshape(N, nb * BLOCK, KV, G, Dh)[:, :L]
    lse = lse.transpose(0, 1, 4, 2, 3).reshape(N, nb * BLOCK, KV, G)[:, :L]
    return o, lse


def combine_by_denominator(outs, lses):
    wts = jax.nn.softmax(jnp.stack(lses, 0), axis=0)
    return jnp.einsum('pbth,pbthd->bthd', wts, jnp.stack(outs, 0))


def dilated_prompt(q, k, v, slopes):
    Bq, S, H, Dh = q.shape
    outs, lses = [], []
    for w, d in DILATIONS:
        def to_res(t):
            return t.reshape(Bq, S // d, d, H, Dh).transpose(0, 2, 1, 3, 4).reshape(Bq * d, S // d, H, Dh)
        o, lse = banded_attention(to_res(q)[:, :, :, None, :], to_res(k), to_res(v), w // d, d,
                                  slopes[:, None], None)
        outs.append(o[:, :, :, 0].reshape(Bq, d, S // d, H, Dh).transpose(0, 2, 1, 3, 4).reshape(Bq, S, H, Dh))
        lses.append(lse[..., 0].reshape(Bq, d, S // d, H).transpose(0, 2, 1, 3).reshape(Bq, S, H))
    return combine_by_denominator(outs, lses)


def dilated_sample(q, k_all, v_all, buf_len, slopes):
    T = q.shape[1]
    outs, lses = [], []
    for w, d in DILATIONS:
        j = jnp.arange(w // d + 1)
        idx = buf_len + jnp.arange(T)[:, None] - d * j[None, :]
        valid = idx >= 0
        idx = jnp.maximum(idx, 0)
        kg = k_all[:, idx]
        vg = v_all[:, idx]
        s = jnp.einsum('bthd,btjhd->bthj', q, kg, preferred_element_type=F32) * ATTN_SCALE
        bias = -(slopes[:, None] * (d * j).astype(F32)[None, :])
        s = jnp.where(valid[None, :, None, :], s + bias, NEG_INF)
        p, lse = masked_softmax(s)
        outs.append(jnp.einsum('bthj,btjhd->bthd', p, vg.astype(F32)))
        lses.append(lse)
    return combine_by_denominator(outs, lses)


def window_sample(q, k_all, v_all, buf_len, slopes, sink):
    T, K = q.shape[1], k_all.shape[1]
    s = jnp.einsum('bthgd,bkhd->bhgtk', q, k_all, preferred_element_type=F32) * ATTN_SCALE
    dist = buf_len + jnp.arange(T)[:, None] - jnp.arange(K)[None, :]
    valid = (dist >= 0) & (dist <= WIN_B)
    s = jnp.where(valid, s - slopes[:, :, None, None] * dist.astype(F32), NEG_INF)
    p, _ = masked_softmax(s, sink.astype(F32)[None, :, :, None, None])
    return jnp.einsum('bhgtk,bkhd->bthgd', p, v_all.astype(F32))


def adaln(c, w, b):
    a = jnp.dot(jax.nn.silu(c), w) + b
    return [t[:, None, :] for t in jnp.split(a, 6, axis=-1)]


def project(u, w):
    Bq, S = u.shape[:2]
    h = jnp.einsum('bsd,dc->bsc', u, w)
    qa, ka, va, qb, kb, vb = jnp.split(h, SPLIT_POINTS, axis=-1)
    return (qa.reshape(Bq, S, H_A, HEAD_DIM), ka.reshape(Bq, S, H_A, HEAD_DIM), va.reshape(Bq, S, H_A, HEAD_DIM),
            qb.reshape(Bq, S, KV_B, G_B, HEAD_DIM), kb.reshape(Bq, S, KV_B, HEAD_DIM), vb.reshape(Bq, S, KV_B, HEAD_DIM))


def merge(oa, ob, ga, gb, w):
    Bq, S = oa.shape[:2]
    ya = rms_norm(oa.reshape(Bq, S, A_WIDTH), ga)
    yb = rms_norm(ob.reshape(Bq, S, B_Q), gb)
    return jnp.einsum('bsc,cd->bsd', jnp.concatenate([ya, yb], -1).astype(w.dtype), w)


def moe(u, router_w, router_b, w_gate, w_up, w_down):
    shp = u.shape
    t = u.reshape(-1, shp[-1])
    aff = jax.nn.sigmoid(jnp.dot(t, router_w, preferred_element_type=F32))
    sel = aff + router_b.astype(F32)
    gscore = lax.top_k(sel.reshape(-1, N_GROUPS, EXPERTS_PER_GROUP), TOP_K)[0].sum(-1)
    gbest = jnp.argmax(gscore, axis=-1)
    in_grp = (jnp.arange(N_EXPERTS) // EXPERTS_PER_GROUP)[None, :] == gbest[:, None]
    _, top_idx = lax.top_k(jnp.where(in_grp, sel, -jnp.inf), TOP_K)
    top_aff = jnp.take_along_axis(aff, top_idx, axis=-1)
    gates = top_aff / jnp.sum(top_aff, -1, keepdims=True)
    comb = jnp.einsum('tk,tke->te', gates, jax.nn.one_hot(top_idx, N_EXPERTS, dtype=F32))
    hg = jnp.einsum('td,edf->tef', t, w_gate)
    hu = jnp.einsum('td,edf->tef', t, w_up)
    act = jax.nn.silu(hg) * hu * comb[:, :, None].astype(t.dtype)
    return jnp.einsum('tef,efd->td', act, w_down).reshape(shp)


def setup_inputs(seed: int = 0) -> dict:
    key = jax.random.key(seed)
    ks = jax.random.split(key, 24)
    nrm = jax.random.normal
    la = min(WIN_A, PAST_LEN)
    lb = min(WIN_B, PAST_LEN)
    col_scale = jnp.concatenate([jnp.ones((2 * A_WIDTH,), F32), jnp.full((A_WIDTH,), BETA, F32),
                                 jnp.ones((B_Q + B_KV,), F32), jnp.full((B_KV,), BETA, F32)])
    return {
        "x_prompt": nrm(ks[0], (BATCH, SEQ, D_MODEL), F32),
        "x_sample": nrm(ks[1], (DEC_BATCH, DEC_SEQ, D_MODEL), F32),
        "cache_a_k": nrm(ks[2], (DEPTH, DEC_BATCH, la, H_A, HEAD_DIM), F32),
        "cache_a_v": nrm(ks[3], (DEPTH, DEC_BATCH, la, H_A, HEAD_DIM), F32) * BETA,
        "cache_b_k": nrm(ks[4], (DEPTH, DEC_BATCH, lb, KV_B, HEAD_DIM), F32),
        "cache_b_v": nrm(ks[5], (DEPTH, DEC_BATCH, lb, KV_B, HEAD_DIM), F32) * BETA,
        "c_prompt": nrm(ks[6], (BATCH, D_MODEL), F32),
        "c_sample": nrm(ks[7], (DEC_BATCH, D_MODEL), F32),
        "w_ada": nrm(ks[8], (DEPTH, D_MODEL, 6 * D_MODEL), F32) * (0.5 * D_MODEL ** -0.5),
        "b_ada": nrm(ks[9], (DEPTH, 6 * D_MODEL), F32) * 0.02,
        "w_in": nrm(ks[10], (DEPTH, D_MODEL, IN_COLS), F32) * (D_MODEL ** -0.5) * col_scale,
        "sinks_b": nrm(ks[11], (DEPTH, H_B), F32) * 0.5,
        "gain_a": 1.0 + 0.02 * nrm(ks[12], (DEPTH, A_WIDTH), F32),
        "gain_b": 1.0 + 0.02 * nrm(ks[13], (DEPTH, B_Q), F32),
        "w_out": nrm(ks[14], (DEPTH, MIX_WIDTH, D_MODEL), F32) * (MIX_WIDTH ** -0.5) * BETA,
        "ln1_g": 1.0 + 0.02 * nrm(ks[15], (DEPTH, D_MODEL), F32),
        "ln1_b": 0.02 * nrm(ks[16], (DEPTH, D_MODEL), F32),
        "ln2_g": 1.0 + 0.02 * nrm(ks[17], (DEPTH, D_MODEL), F32),
        "ln2_b": 0.02 * nrm(ks[18], (DEPTH, D_MODEL), F32),
        "router_w": nrm(ks[19], (D_MODEL, N_EXPERTS), F32) * (D_MODEL ** -0.5),
        "router_b": nrm(ks[20], (N_EXPERTS,), F32) * 0.01,
        "w_gate": nrm(ks[21], (DEPTH, N_EXPERTS, D_MODEL, D_EXPERT), F32) * (D_MODEL ** -0.5),
        "w_up": nrm(ks[22], (DEPTH, N_EXPERTS, D_MODEL, D_EXPERT), F32) * (D_MODEL ** -0.5),
        "w_down": nrm(ks[23], (DEPTH, N_EXPERTS, D_EXPERT, D_MODEL), F32) * (D_EXPERT ** -0.5) * BETA,
    }


def reference(x_prompt, x_sample, cache_a_k, cache_a_v, cache_b_k, cache_b_v, c_prompt, c_sample,
              w_ada, b_ada, w_in, sinks_b, gain_a, gain_b, w_out, ln1_g, ln1_b, ln2_g, ln2_b,
              router_w, router_b, w_gate, w_up, w_down):
    slopes_a = alibi_slopes(H_A)
    slopes_b = alibi_slopes(H_B).reshape(KV_B, G_B)
    S = x_prompt.shape[1]
    la_p, lb_p = min(WIN_A, S), min(WIN_B, S)
    la_s, lb_s = cache_a_k.shape[2], cache_b_k.shape[2]
    xp, xs = x_prompt, x_sample
    pak, pav, pbk, pbv, sak, sav, sbk, sbv = [], [], [], [], [], [], [], []
    for l in range(DEPTH):
        sh1p, sc1p, gt1p, sh2p, sc2p, gt2p = adaln(c_prompt, w_ada[l], b_ada[l])
        sh1s, sc1s, gt1s, sh2s, sc2s, gt2s = adaln(c_sample, w_ada[l], b_ada[l])
        sink_l = sinks_b[l].reshape(KV_B, G_B)
        qa, ka, va, qb, kb, vb = project(xp * (1 + sc1p) + sh1p, w_in[l])
        oa = dilated_prompt(qa, ka, va, slopes_a)
        ob, _ = banded_attention(qb, kb, vb, WIN_B, 1, slopes_b, sink_l)
        xp = layer_norm(ALPHA * xp + gt1p * merge(oa, ob, gain_a[l], gain_b[l], w_out[l]), ln1_g[l], ln1_b[l])
        xp = layer_norm(ALPHA * xp + gt2p * moe(xp * (1 + sc2p) + sh2p, router_w, router_b,
                                                w_gate[l], w_up[l], w_down[l]), ln2_g[l], ln2_b[l])
        pak.append(ka[:, S - la_p:]); pav.append(va[:, S - la_p:])
        pbk.append(kb[:, S - lb_p:]); pbv.append(vb[:, S - lb_p:])
        qa, ka, va, qb, kb, vb = project(xs * (1 + sc1s) + sh1s, w_in[l])
        ka_all = jnp.concatenate([cache_a_k[l], ka], axis=1)
        va_all = jnp.concatenate([cache_a_v[l], va], axis=1)
        kb_all = jnp.concatenate([cache_b_k[l], kb], axis=1)
        vb_all = jnp.concatenate([cache_b_v[l], vb], axis=1)
        oa = dilated_sample(qa, ka_all, va_all, la_s, slopes_a)
        ob = window_sample(qb, kb_all, vb_all, lb_s, slopes_b, sink_l)
        xs = layer_norm(ALPHA * xs + gt1s * merge(oa, ob, gain_a[l], gain_b[l], w_out[l]), ln1_g[l], ln1_b[l])
        xs = layer_norm(ALPHA * xs + gt2s * moe(xs * (1 + sc2s) + sh2s, router_w, router_b,
                                                w_gate[l], w_up[l], w_down[l]), ln2_g[l], ln2_b[l])
        sak.append(ka_all[:, ka_all.shape[1] - la_s:]); sav.append(va_all[:, va_all.shape[1] - la_s:])
        sbk.append(kb_all[:, kb_all.shape[1] - lb_s:]); sbv.append(vb_all[:, vb_all.shape[1] - lb_s:])
    return (xp, xs, jnp.stack(pak), jnp.stack(pav), jnp.stack(pbk), jnp.stack(pbv),
            jnp.stack(sak), jnp.stack(sav), jnp.stack(sbk), jnp.stack(sbv))
```

```python
import functools

import jax
import jax.numpy as jnp
import numpy as np
from jax import lax
from jax.experimental import pallas as pl
from jax.experimental.pallas import tpu as pltpu

F32 = jnp.float32
BF16 = jnp.bfloat16
I32 = jnp.int32

D_MODEL = 1024
DEPTH = 4
HEAD_DIM = 64
N_HEADS = 8
KV_B = 2
G_B = N_HEADS // KV_B
DILATIONS = ((128, 1), (512, 4), (2048, 16))
BAND = 128
WIN_A = 2048
WIN_B = 128
N_EXPERTS = 32
N_GROUPS = 8
EPG = N_EXPERTS // N_GROUPS
D_EXPERT = D_MODEL // 4
ALPHA = (2.0 * DEPTH) ** 0.25
LN_EPS = 1e-5
NEG = -1e30
ATTN_SCALE = HEAD_DIM ** -0.5
A_WIDTH = N_HEADS * HEAD_DIM
IN_COLS = 3 * A_WIDTH + A_WIDTH + 2 * KV_B * HEAD_DIM

LANES = 128
SUBLANES = 8
SUP = BAND * 16
TAIL = LANES
ROW_W = D_MODEL + TAIL


def _silu(x):
    return x * jax.nn.sigmoid(x)


def _layer_norm(z, g, b):
    mu = jnp.mean(z, axis=-1, keepdims=True)
    zc = z - mu
    var = jnp.mean(zc * zc, axis=-1, keepdims=True)
    return zc * lax.rsqrt(var + LN_EPS) * g + b


def _rms_norm(x, g):
    return x * lax.rsqrt(jnp.mean(x * x, axis=-1, keepdims=True) + LN_EPS) * g


def _arb(n):
    return pltpu.CompilerParams(dimension_semantics=("arbitrary",) * n)


ADA_TN = 512


def _ada_body(c_ref, w_ref, b_ref, o_ref):
    a = _silu(c_ref[...]).astype(BF16)
    o_ref[...] = jnp.dot(a, w_ref[...].astype(BF16), preferred_element_type=F32) + b_ref[...]


def _ada(c_all, w_ada, b_ada):
    rows = c_all.shape[0]
    ncol = w_ada.shape[2]
    return pl.pallas_call(
        _ada_body,
        grid=(DEPTH, ncol // ADA_TN),
        in_specs=[
            pl.BlockSpec((rows, D_MODEL), lambda l, j: (0, 0)),
            pl.BlockSpec((None, D_MODEL, ADA_TN), lambda l, j: (l, 0, j)),
            pl.BlockSpec((None, 1, ADA_TN), lambda l, j: (l, 0, j)),
        ],
        out_specs=pl.BlockSpec((None, rows, ADA_TN), lambda l, j: (l, 0, j)),
        out_shape=jax.ShapeDtypeStruct((DEPTH, rows, ncol), F32),
        compiler_params=_arb(2),
        name="ada",
    )(c_all, w_ada, b_ada.reshape(DEPTH, 1, ncol))


def _mod_spec(k, tiles_per_batch, r):
    return pl.BlockSpec((None, r, D_MODEL), lambda i: ((i // tiles_per_batch) * 6 + k, 0, 0))


def _proj_body(x_ref, sh_ref, sc_ref, w_ref, o_ref, wb):
    @pl.when(pl.program_id(0) == 0)
    def _():
        wb[...] = w_ref[...].astype(BF16)

    u = x_ref[...] * (1.0 + sc_ref[...]) + sh_ref[...]
    o_ref[...] = jnp.dot(u.astype(BF16), wb[...], preferred_element_type=F32)


def _proj(x, mod, w_in_l, tr, tiles_per_batch):
    t = x.shape[0]
    r = mod.shape[1]
    return pl.pallas_call(
        _proj_body,
        grid=(t // tr,),
        in_specs=[
            pl.BlockSpec((tr, D_MODEL), lambda i: (i, 0)),
            _mod_spec(0, tiles_per_batch, r),
            _mod_spec(1, tiles_per_batch, r),
            pl.BlockSpec((D_MODEL, IN_COLS), lambda i: (0, 0)),
        ],
        out_specs=pl.BlockSpec((tr, IN_COLS), lambda i: (i, 0)),
        out_shape=jax.ShapeDtypeStruct((t, IN_COLS), F32),
        scratch_shapes=[pltpu.VMEM((D_MODEL, IN_COLS), BF16)],
        compiler_params=_arb(1),
        name="proj",
    )(x, mod, mod, w_in_l)


def _band_unit(q, kw, vw, bias):
    lane = lax.broadcasted_iota(I32, (BAND, LANES), 1)
    qs = q * ATTN_SCALE
    q2 = jnp.concatenate([jnp.where(lane < HEAD_DIM, qs, 0.0), jnp.where(lane >= HEAD_DIM, qs, 0.0)], axis=0)
    s = lax.dot_general(q2.astype(BF16), kw.astype(BF16), (((1,), (1,)), ((), ())), preferred_element_type=F32)
    s = s + bias
    m = jnp.max(s, axis=1, keepdims=True)
    pe = jnp.exp(s - m)
    l = jnp.sum(pe, axis=1, keepdims=True)
    pv = jnp.dot(pe.astype(BF16), vw.astype(BF16), preferred_element_type=F32)
    return pv, m, l


def _pair_merge(a):
    lane = lax.broadcasted_iota(I32, (BAND, LANES), 1)
    return jnp.where(lane < HEAD_DIM, jnp.broadcast_to(a[:BAND], (BAND, LANES)),
                     jnp.broadcast_to(a[BAND:], (BAND, LANES)))


def _make_bias(slope0, slope1, d, first):
    row = lax.broadcasted_iota(I32, (2 * BAND, 2 * BAND), 0)
    col = lax.broadcasted_iota(I32, (2 * BAND, 2 * BAND), 1)
    i = jnp.where(row >= BAND, row - BAND, row)
    dist = i + BAND - col
    valid = (dist >= 0) & (dist <= BAND)
    if first:
        valid = valid & (col >= BAND)
    slope = jnp.where(row >= BAND, slope1, slope0)
    return jnp.where(valid, -(slope * (dist * d).astype(F32)), NEG)


def _attn_a_body(slopes_ref, q_ref, k_ref, v_ref, o_ref, k2, v2, acc, ms, ls, bias):
    p = pl.program_id(1)
    st = pl.program_id(2)

    @pl.when(st == 0)
    def _():
        k2[0:SUP, :] = jnp.zeros((SUP, LANES), F32)
        v2[0:SUP, :] = jnp.zeros((SUP, LANES), F32)
        s0 = slopes_ref[2 * p]
        s1 = slopes_ref[2 * p + 1]
        for ci, (_, d) in enumerate(DILATIONS):
            bias[ci, 0] = _make_bias(s0, s1, d, False)
            bias[ci, 1] = _make_bias(s0, s1, d, True)

    @pl.when(st > 0)
    def _():
        k2[0:SUP, :] = k2[SUP:2 * SUP, :]
        v2[0:SUP, :] = v2[SUP:2 * SUP, :]

    k2[SUP:2 * SUP, :] = k_ref[...]
    v2[SUP:2 * SUP, :] = v_ref[...]

    for ci, (_, d) in enumerate(DILATIONS):
        def unit(n, carry, ci=ci, d=d):
            u = n // d
            r = n - u * d
            qstart = BAND * d * u + r
            kstart = SUP - BAND * d + qstart
            if d == 1:
                qidx = pl.ds(qstart, BAND)
                kidx = pl.ds(kstart, 2 * BAND)
            else:
                qidx = pl.ds(qstart, BAND, stride=d)
                kidx = pl.ds(kstart, 2 * BAND, stride=d)
            first = jnp.logical_and(st == 0, u == 0).astype(I32)
            pv, m, l = _band_unit(q_ref[qidx, :], k2[kidx, :], v2[kidx, :], bias[ci, first])
            acc[ci, qidx, :] = _pair_merge(pv)
            ms[ci, qidx, :] = _pair_merge(m)
            ls[ci, qidx, :] = _pair_merge(l)
            return carry
        lax.fori_loop(0, SUP // BAND, unit, 0)

    def fin(c, carry):
        rows = pl.ds(pl.multiple_of(c * BAND, BAND), BAND)
        m0, m1, m2 = ms[0, rows, :], ms[1, rows, :], ms[2, rows, :]
        mm = jnp.maximum(jnp.maximum(m0, m1), m2)
        w0, w1, w2 = jnp.exp(m0 - mm), jnp.exp(m1 - mm), jnp.exp(m2 - mm)
        den = w0 * ls[0, rows, :] + w1 * ls[1, rows, :] + w2 * ls[2, rows, :]
        num = w0 * acc[0, rows, :] + w1 * acc[1, rows, :] + w2 * acc[2, rows, :]
        o_ref[rows, :] = num / den
        return carry
    lax.fori_loop(0, SUP // BAND, fin, 0)


def _attn_a(h, slopes, nbatch, seq):
    nst = seq // SUP
    npair = N_HEADS // 2

    def col(c0):
        return lambda b, p, st: (b * nst + st, c0 + p)

    return pl.pallas_call(
        _attn_a_body,
        grid=(nbatch, npair, nst),
        in_specs=[
            pl.BlockSpec(memory_space=pltpu.SMEM),
            pl.BlockSpec((SUP, LANES), col(0)),
            pl.BlockSpec((SUP, LANES), col(npair)),
            pl.BlockSpec((SUP, LANES), col(2 * npair)),
        ],
        out_specs=pl.BlockSpec((SUP, LANES), col(0)),
        out_shape=jax.ShapeDtypeStruct((nbatch * seq, A_WIDTH), F32),
        scratch_shapes=[
            pltpu.VMEM((2 * SUP, LANES), F32),
            pltpu.VMEM((2 * SUP, LANES), F32),
            pltpu.VMEM((3, SUP, LANES), F32),
            pltpu.VMEM((3, SUP, LANES), F32),
            pltpu.VMEM((3, SUP, LANES), F32),
            pltpu.VMEM((3, 2, 2 * BAND, 2 * BAND), F32),
        ],
        compiler_params=_arb(3),
        name="attn_a",
    )(slopes, h, h, h)


def _attn_b_body(slopes_ref, sinks_ref, q_ref, k_ref, v_ref, o_ref, k2, v2, bias):
    p = pl.program_id(1)
    st = pl.program_id(2)

    @pl.when(st == 0)
    def _():
        k2[0:BAND, :] = jnp.zeros((BAND, LANES), F32)
        v2[0:BAND, :] = jnp.zeros((BAND, LANES), F32)
        s0 = slopes_ref[2 * p]
        s1 = slopes_ref[2 * p + 1]
        bias[0] = _make_bias(s0, s1, 1, False)
        bias[1] = _make_bias(s0, s1, 1, True)

    @pl.when(st > 0)
    def _():
        k2[0:BAND, :] = k2[SUP:SUP + BAND, :]
        v2[0:BAND, :] = v2[SUP:SUP + BAND, :]

    lane = lax.broadcasted_iota(I32, (SUP, LANES), 1)
    keep = (lane < HEAD_DIM) == (p < G_B // 2)
    k2[BAND:BAND + SUP, :] = jnp.where(keep, k_ref[...], pltpu.roll(k_ref[...], HEAD_DIM, 1))
    v2[BAND:BAND + SUP, :] = jnp.where(keep, v_ref[...], pltpu.roll(v_ref[...], HEAD_DIM, 1))

    row = lax.broadcasted_iota(I32, (2 * BAND, 1), 0)
    sink = jnp.where(row >= BAND, sinks_ref[2 * p + 1], sinks_ref[2 * p])

    def unit(n, carry):
        qstart = pl.multiple_of(n * BAND, BAND)
        first = jnp.logical_and(st == 0, n == 0).astype(I32)
        pv, m, l = _band_unit(q_ref[pl.ds(qstart, BAND), :], k2[pl.ds(qstart, 2 * BAND), :],
                              v2[pl.ds(qstart, 2 * BAND), :], bias[first])
        m2 = jnp.maximum(m, sink)
        scale = jnp.exp(m - m2)
        den = l * scale + jnp.exp(sink - m2)
        o_ref[pl.ds(qstart, BAND), :] = _pair_merge(pv * scale / den)
        return carry
    lax.fori_loop(0, SUP // BAND, unit, 0)


def _attn_b(h, slopes, sinks_l, nbatch, seq):
    nst = seq // SUP
    npair = N_HEADS // 2
    qcol0 = 3 * npair
    kcol = 4 * npair
    vcol = kcol + 1
    return pl.pallas_call(
        _attn_b_body,
        grid=(nbatch, npair, nst),
        in_specs=[
            pl.BlockSpec(memory_space=pltpu.SMEM),
            pl.BlockSpec(memory_space=pltpu.SMEM),
            pl.BlockSpec((SUP, LANES), lambda b, p, st: (b * nst + st, qcol0 + p)),
            pl.BlockSpec((SUP, LANES), lambda b, p, st: (b * nst + st, kcol)),
            pl.BlockSpec((SUP, LANES), lambda b, p, st: (b * nst + st, vcol)),
        ],
        out_specs=pl.BlockSpec((SUP, LANES), lambda b, p, st: (b * nst + st, p)),
        out_shape=jax.ShapeDtypeStruct((nbatch * seq, A_WIDTH), F32),
        scratch_shapes=[
            pltpu.VMEM((BAND + SUP, LANES), F32),
            pltpu.VMEM((BAND + SUP, LANES), F32),
            pltpu.VMEM((2, 2 * BAND, 2 * BAND), F32),
        ],
        compiler_params=_arb(3),
        name="attn_b",
    )(slopes, sinks_l, h, h, h)


def _route(logits_t, rb):
    aff = jax.nn.sigmoid(logits_t)
    sel = aff + rb
    a = [aff[SUBLANES * j:SUBLANES * (j + 1)] for j in range(EPG)]
    s = [sel[SUBLANES * j:SUBLANES * (j + 1)] for j in range(EPG)]
    hi1, lo1 = jnp.maximum(s[0], s[1]), jnp.minimum(s[0], s[1])
    hi2, lo2 = jnp.maximum(s[2], s[3]), jnp.minimum(s[2], s[3])
    top1 = jnp.maximum(hi1, hi2)
    top2 = jnp.maximum(jnp.minimum(hi1, hi2), jnp.maximum(lo1, lo2))
    gscore = top1 + top2
    giota = lax.broadcasted_iota(I32, gscore.shape, 0)
    gmax = jnp.max(gscore, axis=0, keepdims=True)
    gbest = jnp.min(jnp.where(gscore == gmax, giota, N_GROUPS), axis=0, keepdims=True)
    gmask = giota == gbest
    v = [jnp.sum(jnp.where(gmask, sj, 0.0), axis=0, keepdims=True) for sj in s]
    av = [jnp.sum(jnp.where(gmask, aj, 0.0), axis=0, keepdims=True) for aj in a]
    m1 = jnp.maximum(jnp.maximum(v[0], v[1]), jnp.maximum(v[2], v[3]))
    i1 = jnp.where(v[0] == m1, 0, jnp.where(v[1] == m1, 1, jnp.where(v[2] == m1, 2, 3)))
    w = [jnp.where(i1 == j, -jnp.inf, v[j]) for j in range(EPG)]
    m2 = jnp.maximum(jnp.maximum(w[0], w[1]), jnp.maximum(w[2], w[3]))
    i2 = jnp.where(w[0] == m2, 0, jnp.where(w[1] == m2, 1, jnp.where(w[2] == m2, 2, 3)))
    a1 = sum(jnp.where(i1 == j, av[j], 0.0) for j in range(EPG))
    a2 = sum(jnp.where(i2 == j, av[j], 0.0) for j in range(EPG))
    den = a1 + a2
    comb = [jnp.where(i1 == j, a1 / den, 0.0) + jnp.where(i2 == j, a2 / den, 0.0) for j in range(EPG)]
    return gbest, comb


def _merge_body(oa_ref, ob_ref, x_ref, gt_ref, sh_ref, sc_ref, ga_ref, gb_ref, w_ref, lg_ref, lb_ref,
                rw_ref, rb_ref, x1_ref, u2_ref, gid_ref, cnt_ref, wb):
    @pl.when(pl.program_id(0) == 0)
    def _():
        wb[...] = w_ref[...].astype(BF16)
        cnt_ref[...] = jnp.zeros(cnt_ref.shape, F32)

    ya = _rms_norm(oa_ref[...], ga_ref[...])
    yb = _rms_norm(ob_ref[...], gb_ref[...])
    y = jnp.concatenate([ya, yb], axis=-1).astype(BF16)
    mix = jnp.dot(y, wb[...], preferred_element_type=F32)
    x1 = _layer_norm(ALPHA * x_ref[...] + gt_ref[...] * mix, lg_ref[...], lb_ref[...])
    x1_ref[...] = x1
    u2 = x1 * (1.0 + sc_ref[...]) + sh_ref[...]
    u2_ref[:, 0:D_MODEL] = u2

    logits_t = lax.dot_general(rw_ref[...], u2, (((1,), (1,)), ((), ())), precision=lax.Precision.HIGHEST,
                               preferred_element_type=F32)
    gbest, comb = _route(logits_t, rb_ref[...])
    gid_ref[...] = gbest
    n = gbest.shape[1]
    row = lax.broadcasted_iota(I32, (TAIL, n), 0)
    slab = jnp.zeros((TAIL, n), F32)
    for j in range(EPG):
        slab = jnp.where(row == j, comb[j], slab)
    u2_ref[:, D_MODEL:ROW_W] = slab.T
    giota = lax.broadcasted_iota(I32, (N_GROUPS, n), 0)
    cnt_ref[...] += jnp.sum((giota == gbest).astype(F32), axis=1, keepdims=True)


def _merge(oa, ob, x, mod, ga, gb, w_out_l, lg, lb, rw_t, rb, tr, tiles_per_batch):
    t = x.shape[0]
    r = mod.shape[1]
    nt = t // tr
    row = lambda i: (i, 0)
    fixed = lambda i: (0, 0)
    return pl.pallas_call(
        _merge_body,
        grid=(nt,),
        in_specs=[
            pl.BlockSpec((tr, A_WIDTH), row),
            pl.BlockSpec((tr, A_WIDTH), row),
            pl.BlockSpec((tr, D_MODEL), row),
            _mod_spec(2, tiles_per_batch, r),
            _mod_spec(3, tiles_per_batch, r),
            _mod_spec(4, tiles_per_batch, r),
            pl.BlockSpec((1, A_WIDTH), fixed),
            pl.BlockSpec((1, A_WIDTH), fixed),
            pl.BlockSpec((D_MODEL, D_MODEL), fixed),
            pl.BlockSpec((1, D_MODEL), fixed),
            pl.BlockSpec((1, D_MODEL), fixed),
            pl.BlockSpec((N_EXPERTS, D_MODEL), fixed),
            pl.BlockSpec((N_EXPERTS, 1), fixed),
        ],
        out_specs=[
            pl.BlockSpec((tr, D_MODEL), row),
            pl.BlockSpec((tr, ROW_W), row),
            pl.BlockSpec((None, 1, tr), lambda i: (i, 0, 0)),
            pl.BlockSpec((N_GROUPS, LANES), fixed),
        ],
        out_shape=[
            jax.ShapeDtypeStruct((t, D_MODEL), F32),
            jax.ShapeDtypeStruct((t, ROW_W), F32),
            jax.ShapeDtypeStruct((nt, 1, tr), I32),
            jax.ShapeDtypeStruct((N_GROUPS, LANES), F32),
        ],
        scratch_shapes=[pltpu.VMEM((D_MODEL, D_MODEL), BF16)],
        compiler_params=_arb(1),
        name="merge",
    )(oa, ob, x, mod, mod, mod, ga, gb, w_out_l, lg, lb, rw_t, rb)


def _pos_body(gid_ref, cnt_ref, pos_ref, meta_ref, tri, *, tm, chunk):
    t = gid_ref.shape[1]
    shift = int(np.log2(tm))
    cnt = cnt_ref[...].astype(I32)
    padded = ((cnt + (tm - 1)) >> shift) << shift
    giota = lax.broadcasted_iota(I32, (N_GROUPS, LANES), 0)
    off = jnp.zeros((N_GROUPS, LANES), I32)
    for g in range(N_GROUPS - 1):
        off = off + jnp.where(giota > g, padded[g:g + 1, :], 0)
    ends = off + padded
    tile_start = lax.broadcasted_iota(I32, (N_GROUPS, LANES), 1) * tm
    tile_group = jnp.sum((ends <= tile_start).astype(F32), axis=0, keepdims=True).astype(I32)
    tile_group = jnp.minimum(tile_group, N_GROUPS - 1)
    n_used = ends[N_GROUPS - 1:N_GROUPS, :] >> shift
    meta_ref[...] = jnp.where(giota == 0, tile_group, jnp.where(giota == 1, n_used, 0))

    r_i = lax.broadcasted_iota(I32, (chunk, chunk), 0)
    c_i = lax.broadcasted_iota(I32, (chunk, chunk), 1)
    tri[...] = (r_i <= c_i).astype(BF16)
    g8 = lax.broadcasted_iota(I32, (N_GROUPS, chunk), 0)

    def body(c, base):
        cols = pl.ds(pl.multiple_of(c * chunk, chunk), chunk)
        onehot = g8 == gid_ref[:, cols]
        inc = jnp.dot(onehot.astype(BF16), tri[...], preferred_element_type=F32)
        posg = base[:, 0:1] + inc - 1.0
        pos_ref[:, cols] = jnp.sum(jnp.where(onehot, posg, 0.0), axis=0, keepdims=True).astype(I32)
        return base + inc[:, chunk - 1:chunk]
    lax.fori_loop(0, t // chunk, body, off.astype(F32))


def _positions(gid_row, cnt, tm, chunk):
    t = gid_row.shape[1]
    return pl.pallas_call(
        functools.partial(_pos_body, tm=tm, chunk=chunk),
        out_shape=[jax.ShapeDtypeStruct((1, t), I32), jax.ShapeDtypeStruct((N_GROUPS, LANES), I32)],
        scratch_shapes=[pltpu.VMEM((chunk, chunk), BF16)],
        name="positions",
    )(gid_row, cnt)


def _scatter_body(pos_ref, u_ref, dst_in, dst_ref, sem):
    del dst_in
    tr = u_ref.shape[0]

    def issue(r, carry):
        pltpu.make_async_copy(u_ref.at[pl.ds(r, 1), :], dst_ref.at[pl.ds(pos_ref[0, r], 1), :], sem).start()
        return carry
    lax.fori_loop(0, tr, issue, 0)
    pltpu.make_async_copy(u_ref, dst_ref.at[pl.ds(0, tr), :], sem).wait()


def _scatter_rows(pos3, u2c, n_sorted, tr):
    t = u2c.shape[0]
    dst = jnp.zeros((n_sorted, ROW_W), F32)
    return pl.pallas_call(
        _scatter_body,
        grid=(t // tr,),
        in_specs=[
            pl.BlockSpec((None, 1, tr), lambda i: (i, 0, 0), memory_space=pltpu.SMEM),
            pl.BlockSpec((tr, ROW_W), lambda i: (i, 0)),
            pl.BlockSpec(memory_space=pl.ANY),
        ],
        out_specs=pl.BlockSpec(memory_space=pl.ANY),
        out_shape=jax.ShapeDtypeStruct((n_sorted, ROW_W), F32),
        scratch_shapes=[pltpu.SemaphoreType.DMA],
        input_output_aliases={2: 0},
        compiler_params=_arb(1),
        name="scatter_rows",
    )(pos3, u2c, dst)


def _combine_body(pos_ref, ys_ref, x1_ref, gt_ref, lg_ref, lb_ref, o_ref, buf, sem):
    tr = x1_ref.shape[0]

    def issue(r, carry):
        pltpu.make_async_copy(ys_ref.at[pl.ds(pos_ref[0, r], 1), :], buf.at[pl.ds(r, 1), :], sem).start()
        return carry
    lax.fori_loop(0, tr, issue, 0)
    pltpu.make_async_copy(ys_ref.at[pl.ds(0, tr), :], buf, sem).wait()
    o_ref[...] = _layer_norm(ALPHA * x1_ref[...] + gt_ref[...] * buf[...], lg_ref[...], lb_ref[...])


def _combine(pos3, ys, x1, mod, lg, lb, tr, tiles_per_batch):
    t = x1.shape[0]
    r = mod.shape[1]
    return pl.pallas_call(
        _combine_body,
        grid=(t // tr,),
        in_specs=[
            pl.BlockSpec((None, 1, tr), lambda i: (i, 0, 0), memory_space=pltpu.SMEM),
            pl.BlockSpec(memory_space=pl.ANY),
            pl.BlockSpec((tr, D_MODEL), lambda i: (i, 0)),
            _mod_spec(5, tiles_per_batch, r),
            pl.BlockSpec((1, D_MODEL), lambda i: (0, 0)),
            pl.BlockSpec((1, D_MODEL), lambda i: (0, 0)),
        ],
        out_specs=pl.BlockSpec((tr, D_MODEL), lambda i: (i, 0)),
        out_shape=jax.ShapeDtypeStruct((t, D_MODEL), F32),
        scratch_shapes=[pltpu.VMEM((tr, D_MODEL), F32), pltpu.SemaphoreType.DMA],
        compiler_params=_arb(1),
        name="combine",
    )(pos3, ys, x1, mod, lg, lb)


def _moe_body(tg_ref, nu_ref, xs_ref, wg_ref, wu_ref, wd_ref, ys_ref, wgb, wub, wdb):
    i = pl.program_id(0)
    prev = tg_ref[jnp.maximum(i - 1, 0)]

    @pl.when(jnp.logical_or(i == 0, tg_ref[i] != prev))
    def _():
        wgb[...] = wg_ref[...].astype(BF16)
        wub[...] = wu_ref[...].astype(BF16)
        wdb[...] = wd_ref[...].astype(BF16)

    @pl.when(i < nu_ref[0])
    def _():
        x = xs_ref[:, 0:D_MODEL].astype(BF16)
        comb = xs_ref[:, D_MODEL:ROW_W]
        y = jnp.zeros((x.shape[0], D_MODEL), F32)
        for e in range(EPG):
            hg = jnp.dot(x, wgb[e], preferred_element_type=F32)
            hu = jnp.dot(x, wub[e], preferred_element_type=F32)
            act = _silu(hg) * hu * comb[:, e:e + 1]
            y = y + jnp.dot(act.astype(BF16), wdb[e], preferred_element_type=F32)
        ys_ref[...] = y

    @pl.when(i >= nu_ref[0])
    def _():
        ys_ref[...] = jnp.zeros(ys_ref.shape, F32)


def _moe(tile_group, n_used, xs, w_gate, w_up, w_down, layer, tm):
    n_sorted = xs.shape[0]
    nt = n_sorted // tm
    grid_spec = pltpu.PrefetchScalarGridSpec(
        num_scalar_prefetch=2,
        grid=(nt,),
        in_specs=[
            pl.BlockSpec((tm, ROW_W), lambda i, tg, nu: (i, 0)),
            pl.BlockSpec((None, EPG, D_MODEL, D_EXPERT), lambda i, tg, nu: (layer, tg[i], 0, 0)),
            pl.BlockSpec((None, EPG, D_MODEL, D_EXPERT), lambda i, tg, nu: (layer, tg[i], 0, 0)),
            pl.BlockSpec((None, EPG, D_EXPERT, D_MODEL), lambda i, tg, nu: (layer, tg[i], 0, 0)),
        ],
        out_specs=pl.BlockSpec((tm, D_MODEL), lambda i, tg, nu: (i, 0)),
        scratch_shapes=[
            pltpu.VMEM((EPG, D_MODEL, D_EXPERT), BF16),
            pltpu.VMEM((EPG, D_MODEL, D_EXPERT), BF16),
            pltpu.VMEM((EPG, D_EXPERT, D_MODEL), BF16),
        ],
    )
    return pl.pallas_call(
        _moe_body,
        grid_spec=grid_spec,
        out_shape=jax.ShapeDtypeStruct((n_sorted, D_MODEL), F32),
        compiler_params=_arb(1),
        name="moe",
    )(tile_group, n_used, xs, w_gate, w_up, w_down)


def _moe_layer(u2c, gid, cnt, x1, mod, w_gate, w_up, w_down, lg, lb, layer, tr, tiles_per_batch, tm, chunk):
    t = x1.shape[0]
    nt = gid.shape[0]
    n_sorted = (t // tm + N_GROUPS) * tm
    pos, meta = _positions(gid.reshape(1, t), cnt, tm, chunk)
    pos3 = pos.reshape(nt, 1, tr)
    xs = _scatter_rows(pos3, u2c, n_sorted, tr)
    ys = _moe(meta[0, :n_sorted // tm], meta[1, :1], xs, w_gate, w_up, w_down, layer, tm)
    return _combine(pos3, ys, x1, mod, lg, lb, tr, tiles_per_batch)


SA_BB = 4


def _decode_a_cfg(q, kc, vc, kn, vn, slope, d):
    i = lax.broadcasted_iota(I32, (1, BAND, N_HEADS, 1), 1)
    bias = -(slope * ((BAND - i) * d).astype(F32))
    s_c = jnp.sum(q * kc, axis=-1, keepdims=True) * ATTN_SCALE + bias
    s_n = jnp.sum(q * kn, axis=-1, keepdims=True) * ATTN_SCALE
    m = jnp.maximum(jnp.max(s_c, axis=1, keepdims=True), s_n)
    p_c = jnp.exp(s_c - m)
    p_n = jnp.exp(s_n - m)
    l = jnp.sum(p_c, axis=1, keepdims=True) + p_n
    acc = jnp.sum(p_c * vc, axis=1, keepdims=True) + p_n * vn
    return acc, m, l


def _sattn_body(slope_a_ref, slope_b_ref, sink_ref, qa_ref, kna_ref, vna_ref,
                k1_ref, k4_ref, k16_ref, v1_ref, v4_ref, v16_ref,
                qb_ref, knb_ref, vnb_ref, kb_ref, vb_ref, oa_ref, ob_ref):
    bb = qa_ref.shape[0]
    q = qa_ref[...].reshape(bb, 1, N_HEADS, HEAD_DIM)
    kn = kna_ref[...].reshape(bb, 1, N_HEADS, HEAD_DIM)
    vn = vna_ref[...].reshape(bb, 1, N_HEADS, HEAD_DIM)
    slope = slope_a_ref[...].reshape(1, 1, N_HEADS, 1)
    parts = [_decode_a_cfg(q, kr[...], vr[...], kn, vn, slope, d)
             for (kr, vr, (_, d)) in zip((k1_ref, k4_ref, k16_ref), (v1_ref, v4_ref, v16_ref), DILATIONS)]
    mm = jnp.maximum(jnp.maximum(parts[0][1], parts[1][1]), parts[2][1])
    num = sum(jnp.exp(m - mm) * acc for acc, m, _ in parts)
    den = sum(jnp.exp(m - mm) * l for _, m, l in parts)
    oa_ref[...] = (num / den).reshape(bb, N_HEADS, HEAD_DIM)

    qb = qb_ref[...]
    kb = kb_ref[...]
    vb = vb_ref[...]
    knb = knb_ref[...]
    vnb = vnb_ref[...]
    j = lax.broadcasted_iota(I32, (1, 1, WIN_B), 2)
    bias_b = -(slope_b_ref[...].reshape(1, N_HEADS, 1) * (WIN_B - j).astype(F32))
    s_c = jnp.einsum('bhd,bjd->bhj', qb.astype(BF16), kb.astype(BF16), preferred_element_type=F32) * ATTN_SCALE + bias_b
    s_n = jnp.sum(qb * knb, axis=-1, keepdims=True) * ATTN_SCALE
    sink = sink_ref[...].reshape(1, N_HEADS, 1)
    m = jnp.maximum(jnp.maximum(jnp.max(s_c, axis=-1, keepdims=True), s_n), sink)
    p_c = jnp.exp(s_c - m)
    p_n = jnp.exp(s_n - m)
    den_b = jnp.sum(p_c, axis=-1, keepdims=True) + p_n + jnp.exp(sink - m)
    acc_b = jnp.einsum('bhj,bjd->bhd', p_c.astype(BF16), vb.astype(BF16), preferred_element_type=F32) + p_n * vnb
    ob_ref[...] = acc_b / den_b


def _sample_attn(layer, slope_a, slope_b, sink_l, qa, kna, vna, cache_a_k, cache_a_v, qb2, knb, vnb, cbk2, cbv2):
    nb = qa.shape[0]
    la = cache_a_k.shape[2]
    bb = SA_BB
    tile5 = (bb, N_HEADS, HEAD_DIM)
    row3 = lambda i: (i, 0, 0)
    fixed2 = lambda i: (0, 0)

    def views(c):
        out = []
        for (w, d) in DILATIONS:
            v = c.reshape(DEPTH, nb, la // d, d, N_HEADS, HEAD_DIM)
            blk = (la - w) // d // BAND
            spec = pl.BlockSpec((None, bb, BAND, None, N_HEADS, HEAD_DIM),
                                lambda i, blk=blk: (layer, i, blk, 0, 0, 0))
            out.append((v, spec))
        return out

    kv = views(cache_a_k) + views(cache_a_v)
    return pl.pallas_call(
        _sattn_body,
        grid=(nb // bb,),
        in_specs=[
            pl.BlockSpec((N_HEADS, 1), fixed2),
            pl.BlockSpec((N_HEADS, 1), fixed2),
            pl.BlockSpec((N_HEADS, 1), fixed2),
            pl.BlockSpec(tile5, row3), pl.BlockSpec(tile5, row3), pl.BlockSpec(tile5, row3),
            *[s for _, s in kv],
            pl.BlockSpec((bb, N_HEADS, LANES), row3),
            pl.BlockSpec((bb, 1, LANES), row3),
            pl.BlockSpec((bb, 1, LANES), row3),
            pl.BlockSpec((None, bb, WIN_B, LANES), lambda i: (layer, i, 0, 0)),
            pl.BlockSpec((None, bb, WIN_B, LANES), lambda i: (layer, i, 0, 0)),
        ],
        out_specs=[pl.BlockSpec(tile5, row3), pl.BlockSpec((bb, N_HEADS, LANES), row3)],
        out_shape=[jax.ShapeDtypeStruct((nb, N_HEADS, HEAD_DIM), F32),
                   jax.ShapeDtypeStruct((nb, N_HEADS, LANES), F32)],
        compiler_params=_arb(1),
        name="sample_attn",
    )(slope_a, slope_b, sink_l, qa, kna, vna, *[v for v, _ in kv], qb2, knb, vnb, cbk2, cbv2)


CACHE_SPLIT = 8


def _cache_body(*refs):
    n = (len(refs) - 1) // 3
    caches, news, outs, sem = refs[:n], refs[n:2 * n], refs[2 * n:3 * n], refs[3 * n]
    copies = []
    for a in range(n):
        nb, rows = caches[a].shape[1], caches[a].shape[2]
        step = nb // CACHE_SPLIT
        for l in range(DEPTH):
            for c in range(CACHE_SPLIT):
                bs = pl.ds(c * step, step)
                copies.append(pltpu.make_async_copy(caches[a].at[l, bs, pl.ds(1, rows - 1)],
                                                    outs[a].at[l, bs, pl.ds(0, rows - 1)], sem.at[a, l]))
        copies.append(pltpu.make_async_copy(news[a], outs[a].at[:, :, rows - 1], sem.at[a, 0]))
    for cp in copies:
        cp.start()
    for cp in copies:
        cp.wait()


def _cache_update(caches, news):
    n = len(caches)
    any_spec = pl.BlockSpec(memory_space=pl.ANY)
    return pl.pallas_call(
        _cache_body,
        in_specs=[any_spec] * (2 * n),
        out_specs=[any_spec] * n,
        out_shape=[jax.ShapeDtypeStruct(c.shape, c.dtype) for c in caches],
        scratch_shapes=[pltpu.SemaphoreType.DMA((n, DEPTH))],
        name="cache_update",
    )(*caches, *news)


P_TR = 512
P_TR2 = 256
P_TM = 256
P_CHUNK = 512


def kernel(x_prompt, x_sample, cache_a_k, cache_a_v, cache_b_k, cache_b_v, c_prompt, c_sample, w_ada, b_ada, w_in, sinks_b, gain_a, gain_b, w_out, ln1_g, ln1_b, ln2_g, ln2_b, router_w, router_b, w_gate, w_up, w_down):
    nbatch, seq, _ = x_prompt.shape
    nsamp = x_sample.shape[0]
    t_p = nbatch * seq
    slopes = jnp.asarray(2.0 ** (-8.0 * np.arange(1, N_HEADS + 1) / N_HEADS), dtype=F32)
    slopes_col = slopes.reshape(N_HEADS, 1)

    perm = np.array([g * EPG + j for j in range(EPG) for g in range(N_GROUPS)])
    rw_t = router_w.T[perm]
    rb = router_b[perm].reshape(N_EXPERTS, 1)

    pad = (-(nbatch + nsamp)) % SUBLANES
    c_all = jnp.concatenate([c_prompt, c_sample, jnp.zeros((pad, D_MODEL), F32)], axis=0)
    mod_all = _ada(c_all, w_ada, b_ada)

    cbk2 = cache_b_k.reshape(DEPTH, nsamp, WIN_B, KV_B * HEAD_DIM)
    cbv2 = cache_b_v.reshape(DEPTH, nsamp, WIN_B, KV_B * HEAD_DIM)
    head_kv = jnp.asarray((np.arange(N_HEADS) // G_B)[:, None] == np.arange(KV_B)[None, :])
    head_kv = head_kv.reshape(1, N_HEADS, KV_B, 1)

    xp = x_prompt.reshape(t_p, D_MODEL)
    xs = x_sample.reshape(nsamp, D_MODEL)
    pak, pav, pbk, pbv = [], [], [], []
    nka, nva, nkb, nvb = [], [], [], []
    for l in range(DEPTH):
        mod_p = mod_all[l, :nbatch].reshape(nbatch * 6, 1, D_MODEL)
        mod_s = mod_all[l, nbatch:nbatch + nsamp].reshape(nsamp, 6, D_MODEL).transpose(1, 0, 2)
        ga, gb = gain_a[l].reshape(1, A_WIDTH), gain_b[l].reshape(1, A_WIDTH)
        l1g, l1b = ln1_g[l].reshape(1, D_MODEL), ln1_b[l].reshape(1, D_MODEL)
        l2g, l2b = ln2_g[l].reshape(1, D_MODEL), ln2_b[l].reshape(1, D_MODEL)

        h = _proj(xp, mod_p, w_in[l], P_TR, seq // P_TR)
        oa = _attn_a(h, slopes, nbatch, seq)
        ob = _attn_b(h, slopes, sinks_b[l], nbatch, seq)
        x1, u2c, gid, cnt = _merge(oa, ob, xp, mod_p, ga, gb, w_out[l], l1g, l1b, rw_t, rb, P_TR2, seq // P_TR2)
        xp = _moe_layer(u2c, gid, cnt, x1, mod_p, w_gate, w_up, w_down, l2g, l2b, l, P_TR2, seq // P_TR2,
                        P_TM, P_CHUNK)
        h3 = h.reshape(nbatch, seq, IN_COLS)
        pak.append(h3[:, seq - WIN_A:, A_WIDTH:2 * A_WIDTH].reshape(nbatch, WIN_A, N_HEADS, HEAD_DIM))
        pav.append(h3[:, seq - WIN_A:, 2 * A_WIDTH:3 * A_WIDTH].reshape(nbatch, WIN_A, N_HEADS, HEAD_DIM))
        kb0 = 4 * A_WIDTH
        pbk.append(h3[:, seq - WIN_B:, kb0:kb0 + LANES].reshape(nbatch, WIN_B, KV_B, HEAD_DIM))
        pbv.append(h3[:, seq - WIN_B:, kb0 + LANES:kb0 + 2 * LANES].reshape(nbatch, WIN_B, KV_B, HEAD_DIM))

        hs = _proj(xs, mod_s, w_in[l], nsamp, 1)
        qa = hs[:, 0:A_WIDTH].reshape(nsamp, N_HEADS, HEAD_DIM)
        kna = hs[:, A_WIDTH:2 * A_WIDTH].reshape(nsamp, N_HEADS, HEAD_DIM)
        vna = hs[:, 2 * A_WIDTH:3 * A_WIDTH].reshape(nsamp, N_HEADS, HEAD_DIM)
        qb = hs[:, 3 * A_WIDTH:4 * A_WIDTH].reshape(nsamp, N_HEADS, 1, HEAD_DIM)
        qb2 = jnp.where(head_kv, qb, 0.0).reshape(nsamp, N_HEADS, LANES)
        knb = hs[:, kb0:kb0 + LANES]
        vnb = hs[:, kb0 + LANES:kb0 + 2 * LANES]
        oas, obs2 = _sample_attn(l, slopes_col, slopes_col, sinks_b[l].reshape(N_HEADS, 1), qa, kna, vna,
                                 cache_a_k, cache_a_v, qb2, knb.reshape(nsamp, 1, LANES),
                                 vnb.reshape(nsamp, 1, LANES), cbk2, cbv2)
        obs = jnp.sum(jnp.where(head_kv, obs2.reshape(nsamp, N_HEADS, KV_B, HEAD_DIM), 0.0), axis=2)
        x1s, u2cs, gids, cnts = _merge(oas.reshape(nsamp, A_WIDTH), obs.reshape(nsamp, A_WIDTH), xs, mod_s, ga, gb,
                                       w_out[l], l1g, l1b, rw_t, rb, nsamp, 1)
        xs = _moe_layer(u2cs, gids, cnts, x1s, mod_s, w_gate, w_up, w_down, l2g, l2b, l, nsamp, 1, nsamp, nsamp)
        nka.append(kna)
        nva.append(vna)
        nkb.append(knb.reshape(nsamp, KV_B, HEAD_DIM))
        nvb.append(vnb.reshape(nsamp, KV_B, HEAD_DIM))

    sak, sav, sbk, sbv = _cache_update(
        [cache_a_k, cache_a_v, cache_b_k, cache_b_v],
        [jnp.stack(nka), jnp.stack(nva), jnp.stack(nkb), jnp.stack(nvb)])
    return (xp.reshape(nbatch, seq, D_MODEL), xs.reshape(nsamp, 1, D_MODEL),
            jnp.stack(pak), jnp.stack(pav), jnp.stack(pbk), jnp.stack(pbv), sak, sav, sbk, sbv)
```

```python
import functools

import jax
import jax.numpy as jnp
import numpy as np
from jax import lax
from jax.experimental import pallas as pl
from jax.experimental.pallas import tpu as pltpu

F32 = jnp.float32
BF16 = jnp.bfloat16
I32 = jnp.int32

D_MODEL = 1024
DEPTH = 4
HEAD_DIM = 64
N_HEADS = 8
KV_B = 2
G_B = N_HEADS // KV_B
DILATIONS = ((128, 1), (512, 4), (2048, 16))
BAND = 128
WIN_A = 2048
WIN_B = 128
N_EXPERTS = 32
N_GROUPS = 8
EPG = N_EXPERTS // N_GROUPS
D_EXPERT = D_MODEL // 4
ALPHA = (2.0 * DEPTH) ** 0.25
LN_EPS = 1e-5
NEG = -1e30
ATTN_SCALE = HEAD_DIM ** -0.5
A_WIDTH = N_HEADS * HEAD_DIM
IN_COLS = 3 * A_WIDTH + A_WIDTH + 2 * KV_B * HEAD_DIM

LANES = 128
SUBLANES = 8
SUP = BAND * 16
Q4 = SUP // 4
UNROLL = 8
TAIL = LANES
ROW_W = D_MODEL + TAIL


def _silu(x):
    return x * jax.nn.sigmoid(x)


def _layer_norm(z, g, b):
    mu = jnp.mean(z, axis=-1, keepdims=True)
    zc = z - mu
    var = jnp.mean(zc * zc, axis=-1, keepdims=True)
    return zc * lax.rsqrt(var + LN_EPS) * g + b


def _rms_norm(x, g):
    return x * lax.rsqrt(jnp.mean(x * x, axis=-1, keepdims=True) + LN_EPS) * g


def _arb(n):
    return pltpu.CompilerParams(dimension_semantics=("arbitrary",) * n)


ADA_TN = 512


def _ada_body(c_ref, w_ref, b_ref, o_ref):
    a = _silu(c_ref[...]).astype(BF16)
    o_ref[...] = jnp.dot(a, w_ref[...].astype(BF16), preferred_element_type=F32) + b_ref[...]


def _ada(c_all, w_ada, b_ada):
    rows = c_all.shape[0]
    ncol = w_ada.shape[2]
    return pl.pallas_call(
        _ada_body,
        grid=(DEPTH, ncol // ADA_TN),
        in_specs=[
            pl.BlockSpec((rows, D_MODEL), lambda l, j: (0, 0)),
            pl.BlockSpec((None, D_MODEL, ADA_TN), lambda l, j: (l, 0, j)),
            pl.BlockSpec((None, 1, ADA_TN), lambda l, j: (l, 0, j)),
        ],
        out_specs=pl.BlockSpec((None, rows, ADA_TN), lambda l, j: (l, 0, j)),
        out_shape=jax.ShapeDtypeStruct((DEPTH, rows, ncol), F32),
        compiler_params=_arb(2),
        name="ada",
    )(c_all, w_ada, b_ada.reshape(DEPTH, 1, ncol))


def _mod_spec(k, tiles_per_batch, r):
    return pl.BlockSpec((None, r, D_MODEL), lambda i: ((i // tiles_per_batch) * 6 + k, 0, 0))


def _proj_body(x_ref, sh_ref, sc_ref, w_ref, o_ref, wb):
    @pl.when(pl.program_id(0) == 0)
    def _():
        wb[...] = w_ref[...].astype(BF16)

    u = x_ref[...] * (1.0 + sc_ref[...]) + sh_ref[...]
    o_ref[...] = jnp.dot(u.astype(BF16), wb[...], preferred_element_type=F32)


def _proj(x, mod, w_in_l, tr, tiles_per_batch):
    t = x.shape[0]
    r = mod.shape[1]
    return pl.pallas_call(
        _proj_body,
        grid=(t // tr,),
        in_specs=[
            pl.BlockSpec((tr, D_MODEL), lambda i: (i, 0)),
            _mod_spec(0, tiles_per_batch, r),
            _mod_spec(1, tiles_per_batch, r),
            pl.BlockSpec((D_MODEL, IN_COLS), lambda i: (0, 0)),
        ],
        out_specs=pl.BlockSpec((tr, IN_COLS), lambda i: (i, 0)),
        out_shape=jax.ShapeDtypeStruct((t, IN_COLS), F32),
        scratch_shapes=[pltpu.VMEM((D_MODEL, IN_COLS), BF16)],
        compiler_params=_arb(1),
        name="proj",
    )(x, mod, mod, w_in_l)


def _band_unit(q, kw, vw, bias):
    lane = lax.broadcasted_iota(I32, (BAND, LANES), 1)
    qs = q * ATTN_SCALE
    q2 = jnp.concatenate([jnp.where(lane < HEAD_DIM, qs, 0.0), jnp.where(lane >= HEAD_DIM, qs, 0.0)], axis=0)
    s = lax.dot_general(q2.astype(BF16), kw.astype(BF16), (((1,), (1,)), ((), ())), preferred_element_type=F32)
    s = s + bias
    m = jnp.max(s, axis=1, keepdims=True)
    pe = jnp.exp(s - m).astype(BF16)
    vx = jnp.concatenate([vw.astype(BF16), jnp.ones((2 * BAND, LANES), BF16)], axis=1)
    pvx = jnp.dot(pe, vx, preferred_element_type=F32)
    return pvx[:, :LANES], m, pvx[:, LANES:]


def _pair_merge(a):
    lane = lax.broadcasted_iota(I32, (BAND, LANES), 1)
    return jnp.where(lane < HEAD_DIM, jnp.broadcast_to(a[:BAND], (BAND, LANES)),
                     jnp.broadcast_to(a[BAND:], (BAND, LANES)))


def _make_bias(slope0, slope1, d, first):
    row = lax.broadcasted_iota(I32, (2 * BAND, 2 * BAND), 0)
    col = lax.broadcasted_iota(I32, (2 * BAND, 2 * BAND), 1)
    i = jnp.where(row >= BAND, row - BAND, row)
    dist = i + BAND - col
    valid = (dist >= 0) & (dist <= BAND)
    if first:
        valid = valid & (col >= BAND)
    slope = jnp.where(row >= BAND, slope1, slope0)
    return jnp.where(valid, -(slope * (dist * d).astype(F32)), NEG)


def _attn_a_body(slopes_ref, q_ref, k_ref, v_ref, o_ref, qa, ka, va, ktail, vtail, acc, ms, ls, bias):
    p = pl.program_id(1)
    st = pl.program_id(2)
    cur = st % 2
    prv = 1 - cur

    @pl.when(st == 0)
    def _():
        ka[1] = jnp.zeros((4, Q4, LANES), F32)
        va[1] = jnp.zeros((4, Q4, LANES), F32)
        ktail[...] = jnp.zeros((BAND, LANES), F32)
        vtail[...] = jnp.zeros((BAND, LANES), F32)
        s0 = slopes_ref[2 * p]
        s1 = slopes_ref[2 * p + 1]
        for ci, (_, d) in enumerate(DILATIONS):
            bias[ci, 0] = _make_bias(s0, s1, d, False)
            bias[ci, 1] = _make_bias(s0, s1, d, True)

    for c in range(4):
        qa[c] = q_ref[pl.ds(c, Q4, stride=4), :]
        ka[cur, c] = k_ref[pl.ds(c, Q4, stride=4), :]
        va[cur, c] = v_ref[pl.ds(c, Q4, stride=4), :]

    first0 = (st == 0).astype(I32)
    n_groups = SUP // BAND // UNROLL

    def store(ci, idx, pv, m, l):
        acc[ci, idx, :] = _pair_merge(pv)
        ms[ci, idx, :] = _pair_merge(m)
        ls[ci, idx, :] = _pair_merge(l)

    pv, m, l = _band_unit(q_ref[0:BAND, :], jnp.concatenate([ktail[...], k_ref[0:BAND, :]], axis=0),
                          jnp.concatenate([vtail[...], v_ref[0:BAND, :]], axis=0), bias[0, first0])
    store(0, pl.ds(0, BAND), pv, m, l)

    def unit1(n):
        qs = pl.multiple_of(n * BAND, BAND)
        ks = pl.multiple_of(n * BAND - BAND, BAND)
        pv, m, l = _band_unit(q_ref[pl.ds(qs, BAND), :], k_ref[pl.ds(ks, 2 * BAND), :],
                              v_ref[pl.ds(ks, 2 * BAND), :], bias[0, 0])
        store(0, pl.ds(qs, BAND), pv, m, l)

    for n in range(1, UNROLL):
        unit1(n)

    def group1(g, carry):
        for j in range(UNROLL):
            unit1(g * UNROLL + j)
        return carry
    lax.fori_loop(1, n_groups, group1, 0)
    ktail[...] = k_ref[SUP - BAND:SUP, :]
    vtail[...] = v_ref[SUP - BAND:SUP, :]

    def unit4(u, r, first):
        qs = pl.multiple_of(u * BAND, BAND)
        if first:
            kw = jnp.concatenate([ka[prv, r, Q4 - BAND:Q4, :], ka[cur, r, 0:BAND, :]], axis=0)
            vw = jnp.concatenate([va[prv, r, Q4 - BAND:Q4, :], va[cur, r, 0:BAND, :]], axis=0)
            b = bias[1, first0]
        else:
            kw = ka[cur, r, pl.ds(qs - BAND, 2 * BAND), :]
            vw = va[cur, r, pl.ds(qs - BAND, 2 * BAND), :]
            b = bias[1, 0]
        pv, m, l = _band_unit(qa[r, pl.ds(qs, BAND), :], kw, vw, b)
        store(1, pl.ds(r * Q4 + qs, BAND), pv, m, l)

    for n in range(UNROLL):
        unit4(n // 4, n % 4, n < 4)

    def group4(g, carry):
        for j in range(UNROLL):
            unit4(g * (UNROLL // 4) + j // 4, j % 4, False)
        return carry
    lax.fori_loop(1, n_groups, group4, 0)

    def group16(g, carry):
        for j in range(UNROLL):
            c = g * (UNROLL // 4) + j // 4
            a = j % 4
            idx = pl.ds(a, BAND, stride=4)
            kw = jnp.concatenate([ka[prv, c, idx, :], ka[cur, c, idx, :]], axis=0)
            vw = jnp.concatenate([va[prv, c, idx, :], va[cur, c, idx, :]], axis=0)
            pv, m, l = _band_unit(qa[c, idx, :], kw, vw, bias[2, first0])
            store(2, pl.ds(c * Q4 + a, BAND, stride=4), pv, m, l)
        return carry
    lax.fori_loop(0, n_groups, group16, 0)

    def fin(i, carry):
        c = i // 4
        rows = pl.ds(pl.multiple_of(i * BAND, BAND), BAND)
        tok = pl.ds(c + 4 * BAND * (i - 4 * c), BAND, stride=4)
        m0, m1, m2 = ms[0, tok, :], ms[1, rows, :], ms[2, rows, :]
        mm = jnp.maximum(jnp.maximum(m0, m1), m2)
        w0, w1, w2 = jnp.exp(m0 - mm), jnp.exp(m1 - mm), jnp.exp(m2 - mm)
        den = w0 * ls[0, tok, :] + w1 * ls[1, rows, :] + w2 * ls[2, rows, :]
        num = w0 * acc[0, tok, :] + w1 * acc[1, rows, :] + w2 * acc[2, rows, :]
        o_ref[tok, :] = num / den
        return carry
    lax.fori_loop(0, SUP // BAND, fin, 0)


def _attn_a(h, slopes, nbatch, seq):
    assert UNROLL % 4 == 0 and (SUP // BAND) % UNROLL == 0 and seq % SUP == 0
    nst = seq // SUP
    npair = N_HEADS // 2

    def col(c0):
        return lambda b, p, st: (b * nst + st, c0 + p)

    return pl.pallas_call(
        _attn_a_body,
        grid=(nbatch, npair, nst),
        in_specs=[
            pl.BlockSpec(memory_space=pltpu.SMEM),
            pl.BlockSpec((SUP, LANES), col(0)),
            pl.BlockSpec((SUP, LANES), col(npair)),
            pl.BlockSpec((SUP, LANES), col(2 * npair)),
        ],
        out_specs=pl.BlockSpec((SUP, LANES), col(0)),
        out_shape=jax.ShapeDtypeStruct((nbatch * seq, A_WIDTH), F32),
        scratch_shapes=[
            pltpu.VMEM((4, Q4, LANES), F32),
            pltpu.VMEM((2, 4, Q4, LANES), F32),
            pltpu.VMEM((2, 4, Q4, LANES), F32),
            pltpu.VMEM((BAND, LANES), F32),
            pltpu.VMEM((BAND, LANES), F32),
            pltpu.VMEM((3, SUP, LANES), F32),
            pltpu.VMEM((3, SUP, LANES), F32),
            pltpu.VMEM((3, SUP, LANES), F32),
            pltpu.VMEM((3, 2, 2 * BAND, 2 * BAND), F32),
        ],
        compiler_params=_arb(3),
        name="attn_a",
    )(slopes, h, h, h)


def _attn_b_body(slopes_ref, sinks_ref, q_ref, k_ref, v_ref, o_ref, k2, v2, bias):
    p = pl.program_id(1)
    st = pl.program_id(2)

    @pl.when(st == 0)
    def _():
        k2[0:BAND, :] = jnp.zeros((BAND, LANES), F32)
        v2[0:BAND, :] = jnp.zeros((BAND, LANES), F32)
        s0 = slopes_ref[2 * p]
        s1 = slopes_ref[2 * p + 1]
        bias[0] = _make_bias(s0, s1, 1, False)
        bias[1] = _make_bias(s0, s1, 1, True)

    @pl.when(st > 0)
    def _():
        k2[0:BAND, :] = k2[SUP:SUP + BAND, :]
        v2[0:BAND, :] = v2[SUP:SUP + BAND, :]

    lane = lax.broadcasted_iota(I32, (SUP, LANES), 1)
    keep = (lane < HEAD_DIM) == (p < G_B // 2)
    k2[BAND:BAND + SUP, :] = jnp.where(keep, k_ref[...], pltpu.roll(k_ref[...], HEAD_DIM, 1))
    v2[BAND:BAND + SUP, :] = jnp.where(keep, v_ref[...], pltpu.roll(v_ref[...], HEAD_DIM, 1))

    row = lax.broadcasted_iota(I32, (2 * BAND, 1), 0)
    sink = jnp.where(row >= BAND, sinks_ref[2 * p + 1], sinks_ref[2 * p])

    def group(g, carry):
        for j in range(UNROLL):
            n = g * UNROLL + j
            qstart = pl.multiple_of(n * BAND, BAND)
            first = jnp.logical_and(st == 0, n == 0).astype(I32)
            pv, m, l = _band_unit(q_ref[pl.ds(qstart, BAND), :], k2[pl.ds(qstart, 2 * BAND), :],
                                  v2[pl.ds(qstart, 2 * BAND), :], bias[first])
            m2 = jnp.maximum(m, sink)
            scale = jnp.exp(m - m2)
            den = l * scale + jnp.exp(sink - m2)
            o_ref[pl.ds(qstart, BAND), :] = _pair_merge(pv * scale / den)
        return carry
    lax.fori_loop(0, SUP // BAND // UNROLL, group, 0)


def _attn_b(h, slopes, sinks_l, nbatch, seq):
    nst = seq // SUP
    npair = N_HEADS // 2
    qcol0 = 3 * npair
    kcol = 4 * npair
    vcol = kcol + 1
    return pl.pallas_call(
        _attn_b_body,
        grid=(nbatch, npair, nst),
        in_specs=[
            pl.BlockSpec(memory_space=pltpu.SMEM),
            pl.BlockSpec(memory_space=pltpu.SMEM),
            pl.BlockSpec((SUP, LANES), lambda b, p, st: (b * nst + st, qcol0 + p)),
            pl.BlockSpec((SUP, LANES), lambda b, p, st: (b * nst + st, kcol)),
            pl.BlockSpec((SUP, LANES), lambda b, p, st: (b * nst + st, vcol)),
        ],
        out_specs=pl.BlockSpec((SUP, LANES), lambda b, p, st: (b * nst + st, p)),
        out_shape=jax.ShapeDtypeStruct((nbatch * seq, A_WIDTH), F32),
        scratch_shapes=[
            pltpu.VMEM((BAND + SUP, LANES), F32),
            pltpu.VMEM((BAND + SUP, LANES), F32),
            pltpu.VMEM((2, 2 * BAND, 2 * BAND), F32),
        ],
        compiler_params=_arb(3),
        name="attn_b",
    )(slopes, sinks_l, h, h, h)


def _route(logits_t, rb):
    aff = jax.nn.sigmoid(logits_t)
    sel = aff + rb
    a = [aff[SUBLANES * j:SUBLANES * (j + 1)] for j in range(EPG)]
    s = [sel[SUBLANES * j:SUBLANES * (j + 1)] for j in range(EPG)]
    hi1, lo1 = jnp.maximum(s[0], s[1]), jnp.minimum(s[0], s[1])
    hi2, lo2 = jnp.maximum(s[2], s[3]), jnp.minimum(s[2], s[3])
    top1 = jnp.maximum(hi1, hi2)
    top2 = jnp.maximum(jnp.minimum(hi1, hi2), jnp.maximum(lo1, lo2))
    gscore = top1 + top2
    giota = lax.broadcasted_iota(I32, gscore.shape, 0)
    gmax = jnp.max(gscore, axis=0, keepdims=True)
    gbest = jnp.min(jnp.where(gscore == gmax, giota, N_GROUPS), axis=0, keepdims=True)
    gmask = giota == gbest
    v = [jnp.sum(jnp.where(gmask, sj, 0.0), axis=0, keepdims=True) for sj in s]
    av = [jnp.sum(jnp.where(gmask, aj, 0.0), axis=0, keepdims=True) for aj in a]
    m1 = jnp.maximum(jnp.maximum(v[0], v[1]), jnp.maximum(v[2], v[3]))
    i1 = jnp.where(v[0] == m1, 0, jnp.where(v[1] == m1, 1, jnp.where(v[2] == m1, 2, 3)))
    w = [jnp.where(i1 == j, -jnp.inf, v[j]) for j in range(EPG)]
    m2 = jnp.maximum(jnp.maximum(w[0], w[1]), jnp.maximum(w[2], w[3]))
    i2 = jnp.where(w[0] == m2, 0, jnp.where(w[1] == m2, 1, jnp.where(w[2] == m2, 2, 3)))
    a1 = sum(jnp.where(i1 == j, av[j], 0.0) for j in range(EPG))
    a2 = sum(jnp.where(i2 == j, av[j], 0.0) for j in range(EPG))
    den = a1 + a2
    comb = [jnp.where(i1 == j, a1 / den, 0.0) + jnp.where(i2 == j, a2 / den, 0.0) for j in range(EPG)]
    return gbest, comb


def _merge_body(oa_ref, ob_ref, x_ref, gt_ref, sh_ref, sc_ref, ga_ref, gb_ref, w_ref, lg_ref, lb_ref,
                rw_ref, rb_ref, x1_ref, u2_ref, gid_ref, cnt_ref, wb):
    @pl.when(pl.program_id(0) == 0)
    def _():
        wb[...] = w_ref[...].astype(BF16)
        cnt_ref[...] = jnp.zeros(cnt_ref.shape, F32)

    ya = _rms_norm(oa_ref[...], ga_ref[...])
    yb = _rms_norm(ob_ref[...], gb_ref[...])
    y = jnp.concatenate([ya, yb], axis=-1).astype(BF16)
    mix = jnp.dot(y, wb[...], preferred_element_type=F32)
    x1 = _layer_norm(ALPHA * x_ref[...] + gt_ref[...] * mix, lg_ref[...], lb_ref[...])
    x1_ref[...] = x1
    u2 = x1 * (1.0 + sc_ref[...]) + sh_ref[...]
    u2_ref[:, 0:D_MODEL] = u2

    logits_t = lax.dot_general(rw_ref[...], u2, (((1,), (1,)), ((), ())), precision=lax.Precision.HIGHEST,
                               preferred_element_type=F32)
    gbest, comb = _route(logits_t, rb_ref[...])
    gid_ref[...] = gbest
    n = gbest.shape[1]
    row = lax.broadcasted_iota(I32, (TAIL, n), 0)
    slab = jnp.zeros((TAIL, n), F32)
    for j in range(EPG):
        slab = jnp.where(row == j, comb[j], slab)
    u2_ref[:, D_MODEL:ROW_W] = slab.T
    giota = lax.broadcasted_iota(I32, (N_GROUPS, n), 0)
    cnt_ref[...] += jnp.sum((giota == gbest).astype(F32), axis=1, keepdims=True)


def _merge(oa, ob, x, mod, ga, gb, w_out_l, lg, lb, rw_t, rb, tr, tiles_per_batch):
    t = x.shape[0]
    r = mod.shape[1]
    nt = t // tr
    row = lambda i: (i, 0)
    fixed = lambda i: (0, 0)
    return pl.pallas_call(
        _merge_body,
        grid=(nt,),
        in_specs=[
            pl.BlockSpec((tr, A_WIDTH), row),
            pl.BlockSpec((tr, A_WIDTH), row),
            pl.BlockSpec((tr, D_MODEL), row),
            _mod_spec(2, tiles_per_batch, r),
            _mod_spec(3, tiles_per_batch, r),
            _mod_spec(4, tiles_per_batch, r),
            pl.BlockSpec((1, A_WIDTH), fixed),
            pl.BlockSpec((1, A_WIDTH), fixed),
            pl.BlockSpec((D_MODEL, D_MODEL), fixed),
            pl.BlockSpec((1, D_MODEL), fixed),
            pl.BlockSpec((1, D_MODEL), fixed),
            pl.BlockSpec((N_EXPERTS, D_MODEL), fixed),
            pl.BlockSpec((N_EXPERTS, 1), fixed),
        ],
        out_specs=[
            pl.BlockSpec((tr, D_MODEL), row),
            pl.BlockSpec((tr, ROW_W), row),
            pl.BlockSpec((None, 1, tr), lambda i: (i, 0, 0)),
            pl.BlockSpec((N_GROUPS, LANES), fixed),
        ],
        out_shape=[
            jax.ShapeDtypeStruct((t, D_MODEL), F32),
            jax.ShapeDtypeStruct((t, ROW_W), F32),
            jax.ShapeDtypeStruct((nt, 1, tr), I32),
            jax.ShapeDtypeStruct((N_GROUPS, LANES), F32),
        ],
        scratch_shapes=[pltpu.VMEM((D_MODEL, D_MODEL), BF16)],
        compiler_params=_arb(1),
        name="merge",
    )(oa, ob, x, mod, mod, mod, ga, gb, w_out_l, lg, lb, rw_t, rb)


def _pos_body(gid_ref, cnt_ref, pos_ref, meta_ref, tri, *, tm, chunk):
    t = gid_ref.shape[1]
    shift = int(np.log2(tm))
    cnt = cnt_ref[...].astype(I32)
    padded = ((cnt + (tm - 1)) >> shift) << shift
    giota = lax.broadcasted_iota(I32, (N_GROUPS, LANES), 0)
    off = jnp.zeros((N_GROUPS, LANES), I32)
    for g in range(N_GROUPS - 1):
        off = off + jnp.where(giota > g, padded[g:g + 1, :], 0)
    ends = off + padded
    tile_start = lax.broadcasted_iota(I32, (N_GROUPS, LANES), 1) * tm
    tile_group = jnp.sum((ends <= tile_start).astype(F32), axis=0, keepdims=True).astype(I32)
    tile_group = jnp.minimum(tile_group, N_GROUPS - 1)
    n_used = ends[N_GROUPS - 1:N_GROUPS, :] >> shift
    meta_ref[...] = jnp.where(giota == 0, tile_group, jnp.where(giota == 1, n_used, 0))

    r_i = lax.broadcasted_iota(I32, (chunk, chunk), 0)
    c_i = lax.broadcasted_iota(I32, (chunk, chunk), 1)
    tri[...] = (r_i <= c_i).astype(BF16)
    g8 = lax.broadcasted_iota(I32, (N_GROUPS, chunk), 0)

    def body(c, base):
        cols = pl.ds(pl.multiple_of(c * chunk, chunk), chunk)
        onehot = g8 == gid_ref[:, cols]
        inc = jnp.dot(onehot.astype(BF16), tri[...], preferred_element_type=F32)
        posg = base[:, 0:1] + inc - 1.0
        pos_ref[:, cols] = jnp.sum(jnp.where(onehot, posg, 0.0), axis=0, keepdims=True).astype(I32)
        return base + inc[:, chunk - 1:chunk]
    lax.fori_loop(0, t // chunk, body, off.astype(F32))


def _positions(gid_row, cnt, tm, chunk):
    t = gid_row.shape[1]
    return pl.pallas_call(
        functools.partial(_pos_body, tm=tm, chunk=chunk),
        out_shape=[jax.ShapeDtypeStruct((1, t), I32), jax.ShapeDtypeStruct((N_GROUPS, LANES), I32)],
        scratch_shapes=[pltpu.VMEM((chunk, chunk), BF16)],
        name="positions",
    )(gid_row, cnt)


def _scatter_body(pos_ref, u_ref, dst_in, dst_ref, sem):
    del dst_in
    tr = u_ref.shape[0]

    def issue(r, carry):
        pltpu.make_async_copy(u_ref.at[pl.ds(r, 1), :], dst_ref.at[pl.ds(pos_ref[0, r], 1), :], sem).start()
        return carry
    lax.fori_loop(0, tr, issue, 0)
    pltpu.make_async_copy(u_ref, dst_ref.at[pl.ds(0, tr), :], sem).wait()


def _scatter_rows(pos3, u2c, n_sorted, tr):
    t = u2c.shape[0]
    dst = jnp.zeros((n_sorted, ROW_W), F32)
    return pl.pallas_call(
        _scatter_body,
        grid=(t // tr,),
        in_specs=[
            pl.BlockSpec((None, 1, tr), lambda i: (i, 0, 0), memory_space=pltpu.SMEM),
            pl.BlockSpec((tr, ROW_W), lambda i: (i, 0)),
            pl.BlockSpec(memory_space=pl.ANY),
        ],
        out_specs=pl.BlockSpec(memory_space=pl.ANY),
        out_shape=jax.ShapeDtypeStruct((n_sorted, ROW_W), F32),
        scratch_shapes=[pltpu.SemaphoreType.DMA],
        input_output_aliases={2: 0},
        compiler_params=_arb(1),
        name="scatter_rows",
    )(pos3, u2c, dst)


def _combine_body(pos_ref, ys_ref, x1_ref, gt_ref, lg_ref, lb_ref, o_ref, buf, sem):
    tr = x1_ref.shape[0]

    def issue(r, carry):
        pltpu.make_async_copy(ys_ref.at[pl.ds(pos_ref[0, r], 1), :], buf.at[pl.ds(r, 1), :], sem).start()
        return carry
    lax.fori_loop(0, tr, issue, 0)
    pltpu.make_async_copy(ys_ref.at[pl.ds(0, tr), :], buf, sem).wait()
    o_ref[...] = _layer_norm(ALPHA * x1_ref[...] + gt_ref[...] * buf[...], lg_ref[...], lb_ref[...])


def _combine(pos3, ys, x1, mod, lg, lb, tr, tiles_per_batch):
    t = x1.shape[0]
    r = mod.shape[1]
    return pl.pallas_call(
        _combine_body,
        grid=(t // tr,),
        in_specs=[
            pl.BlockSpec((None, 1, tr), lambda i: (i, 0, 0), memory_space=pltpu.SMEM),
            pl.BlockSpec(memory_space=pl.ANY),
            pl.BlockSpec((tr, D_MODEL), lambda i: (i, 0)),
            _mod_spec(5, tiles_per_batch, r),
            pl.BlockSpec((1, D_MODEL), lambda i: (0, 0)),
            pl.BlockSpec((1, D_MODEL), lambda i: (0, 0)),
        ],
        out_specs=pl.BlockSpec((tr, D_MODEL), lambda i: (i, 0)),
        out_shape=jax.ShapeDtypeStruct((t, D_MODEL), F32),
        scratch_shapes=[pltpu.VMEM((tr, D_MODEL), F32), pltpu.SemaphoreType.DMA],
        compiler_params=_arb(1),
        name="combine",
    )(pos3, ys, x1, mod, lg, lb)


def _moe_body(tg_ref, nu_ref, xs_ref, wg_ref, wu_ref, wd_ref, ys_ref, wgb, wub, wdb):
    i = pl.program_id(0)
    prev = tg_ref[jnp.maximum(i - 1, 0)]

    @pl.when(jnp.logical_or(i == 0, tg_ref[i] != prev))
    def _():
        wgb[...] = wg_ref[...].astype(BF16)
        wub[...] = wu_ref[...].astype(BF16)
        wdb[...] = wd_ref[...].astype(BF16)

    @pl.when(i < nu_ref[0])
    def _():
        x = xs_ref[:, 0:D_MODEL].astype(BF16)
        comb = xs_ref[:, D_MODEL:ROW_W]
        y = jnp.zeros((x.shape[0], D_MODEL), F32)
        for e in range(EPG):
            hg = jnp.dot(x, wgb[e], preferred_element_type=F32)
            hu = jnp.dot(x, wub[e], preferred_element_type=F32)
            act = _silu(hg) * hu * comb[:, e:e + 1]
            y = y + jnp.dot(act.astype(BF16), wdb[e], preferred_element_type=F32)
        ys_ref[...] = y

    @pl.when(i >= nu_ref[0])
    def _():
        ys_ref[...] = jnp.zeros(ys_ref.shape, F32)


def _moe(tile_group, n_used, xs, w_gate, w_up, w_down, layer, tm):
    n_sorted = xs.shape[0]
    nt = n_sorted // tm
    grid_spec = pltpu.PrefetchScalarGridSpec(
        num_scalar_prefetch=2,
        grid=(nt,),
        in_specs=[
            pl.BlockSpec((tm, ROW_W), lambda i, tg, nu: (i, 0)),
            pl.BlockSpec((None, EPG, D_MODEL, D_EXPERT), lambda i, tg, nu: (layer, tg[i], 0, 0)),
            pl.BlockSpec((None, EPG, D_MODEL, D_EXPERT), lambda i, tg, nu: (layer, tg[i], 0, 0)),
            pl.BlockSpec((None, EPG, D_EXPERT, D_MODEL), lambda i, tg, nu: (layer, tg[i], 0, 0)),
        ],
        out_specs=pl.BlockSpec((tm, D_MODEL), lambda i, tg, nu: (i, 0)),
        scratch_shapes=[
            pltpu.VMEM((EPG, D_MODEL, D_EXPERT), BF16),
            pltpu.VMEM((EPG, D_MODEL, D_EXPERT), BF16),
            pltpu.VMEM((EPG, D_EXPERT, D_MODEL), BF16),
        ],
    )
    return pl.pallas_call(
        _moe_body,
        grid_spec=grid_spec,
        out_shape=jax.ShapeDtypeStruct((n_sorted, D_MODEL), F32),
        compiler_params=_arb(1),
        name="moe",
    )(tile_group, n_used, xs, w_gate, w_up, w_down)


def _moe_layer(u2c, gid, cnt, x1, mod, w_gate, w_up, w_down, lg, lb, layer, tr, tiles_per_batch, tm, chunk):
    t = x1.shape[0]
    nt = gid.shape[0]
    n_sorted = (t // tm + N_GROUPS) * tm
    pos, meta = _positions(gid.reshape(1, t), cnt, tm, chunk)
    pos3 = pos.reshape(nt, 1, tr)
    xs = _scatter_rows(pos3, u2c, n_sorted, tr)
    ys = _moe(meta[0, :n_sorted // tm], meta[1, :1], xs, w_gate, w_up, w_down, layer, tm)
    return _combine(pos3, ys, x1, mod, lg, lb, tr, tiles_per_batch)


def _shift_in(x, new_col):
    n = x.shape[-1]
    lane = lax.broadcasted_iota(I32, x.shape, x.ndim - 1)
    return jnp.where(lane == n - 1, new_col, pltpu.roll(x, n - 1, x.ndim - 1))


def _decode(q_t, k_of, v_of, kn_t, vn_t, head_kv, slope, mult, mult_new, sink):
    n = mult.shape[1]
    rows, rows_new = [], []
    for h in range(N_HEADS):
        g = head_kv[h]
        qh = q_t[:, h:h + 1]
        rows.append(jnp.sum(qh * k_of(g), axis=0, keepdims=True))
        rows_new.append(jnp.sum(qh * kn_t[:, g:g + 1], axis=0, keepdims=True))
    pos = lax.broadcasted_iota(I32, (1, n), 1)
    s = jnp.concatenate(rows, axis=0) * ATTN_SCALE - slope * (n - pos).astype(F32)
    s = jnp.where(mult > 0.0, s, NEG)
    s_new = jnp.concatenate(rows_new, axis=0) * ATTN_SCALE
    m = jnp.maximum(jnp.max(s, axis=1, keepdims=True), s_new)
    if sink is not None:
        m = jnp.maximum(m, sink)
    p = mult * jnp.exp(s - m)
    p_new = mult_new * jnp.exp(s_new - m)
    den = jnp.sum(p, axis=1, keepdims=True) + p_new
    if sink is not None:
        den = den + jnp.exp(sink - m)
    w = p / den
    w_new = p_new / den
    cols = []
    for h in range(N_HEADS):
        g = head_kv[h]
        cols.append(jnp.sum(w[h:h + 1, :] * v_of(g), axis=1, keepdims=True) + w_new[h:h + 1, :] * vn_t[:, g:g + 1])
    return jnp.concatenate(cols, axis=1)


def _dilated_multiplicity(n):
    pos = lax.broadcasted_iota(I32, (1, n), 1)
    dist = n - pos
    mult = jnp.zeros((1, n), F32)
    for (w, d) in DILATIONS:
        mult = mult + jnp.where((dist <= w) & ((dist & (d - 1)) == 0), 1.0, 0.0)
    return mult


N_SAMPLE_IN = 12


def _sample_body(*refs):
    (slope_ref, sink_ref, qa_ref, kna_ref, vna_ref, qb_ref, knb_ref, vnb_ref,
     ka_ref, va_ref, kb_ref, vb_ref) = refs[:N_SAMPLE_IN]
    oa_ref, ob_ref, oka_ref, ova_ref, okb_ref, ovb_ref = refs[-6:]

    slope = slope_ref[...]
    kna, vna = kna_ref[...], vna_ref[...]
    oa_ref[...] = _decode(qa_ref[...], lambda g: ka_ref[g], lambda g: va_ref[g], kna, vna,
                          list(range(N_HEADS)), slope, _dilated_multiplicity(WIN_A), float(len(DILATIONS)), None)
    knb, vnb = knb_ref[...], vnb_ref[...]
    ob_ref[...] = _decode(qb_ref[...], lambda g: kb_ref[g], lambda g: vb_ref[g], knb, vnb,
                          [h // G_B for h in range(N_HEADS)], slope, jnp.ones((1, WIN_B), F32), 1.0, sink_ref[...])

    def shift(h, carry):
        hot = lax.broadcasted_iota(I32, (1, N_HEADS), 1) == h
        oka_ref[h] = _shift_in(ka_ref[h], jnp.sum(jnp.where(hot, kna, 0.0), axis=1, keepdims=True))
        ova_ref[h] = _shift_in(va_ref[h], jnp.sum(jnp.where(hot, vna, 0.0), axis=1, keepdims=True))
        return carry
    lax.fori_loop(0, N_HEADS, shift, 0)
    for g in range(KV_B):
        okb_ref[g] = _shift_in(kb_ref[g], knb[:, g:g + 1])
        ovb_ref[g] = _shift_in(vb_ref[g], vnb[:, g:g + 1])


def _sample_attn(layer, slope, sink_l, qa, kna, vna, qb, knb, vnb, caches, prev_outs):
    nb = qa.shape[0]
    assert caches[0].shape[-1] == WIN_A and caches[2].shape[-1] == WIN_B
    col = lambda hh: pl.BlockSpec((None, HEAD_DIM, hh), lambda i: (i, 0, 0))
    cache = lambda hh, n: pl.BlockSpec((None, None, hh, HEAD_DIM, n), lambda i: (layer, i, 0, 0, 0))
    small = pl.BlockSpec((N_HEADS, 1), lambda i: (0, 0))
    in_specs = [small, small, col(N_HEADS), col(N_HEADS), col(N_HEADS), col(N_HEADS), col(KV_B), col(KV_B),
                cache(N_HEADS, WIN_A), cache(N_HEADS, WIN_A), cache(KV_B, WIN_B), cache(KV_B, WIN_B)]
    args = [slope, sink_l, qa, kna, vna, qb, knb, vnb, *caches]
    assert len(args) == N_SAMPLE_IN
    aliases = {}
    if prev_outs is not None:
        in_specs += [pl.BlockSpec(memory_space=pl.ANY)] * 4
        args += list(prev_outs)
        aliases = {N_SAMPLE_IN + k: 2 + k for k in range(4)}
    return pl.pallas_call(
        _sample_body,
        grid=(nb,),
        in_specs=in_specs,
        out_specs=[col(N_HEADS), col(N_HEADS), cache(N_HEADS, WIN_A), cache(N_HEADS, WIN_A),
                   cache(KV_B, WIN_B), cache(KV_B, WIN_B)],
        out_shape=[jax.ShapeDtypeStruct((nb, HEAD_DIM, N_HEADS), F32)] * 2 + [
            jax.ShapeDtypeStruct(c.shape, F32) for c in caches],
        input_output_aliases=aliases,
        compiler_params=_arb(1),
        name="sample_attn",
    )(*args)


P_TR = 512
P_TR2 = 256
P_TM = 256
P_CHUNK = 512


def kernel(x_prompt, x_sample, cache_a_k, cache_a_v, cache_b_k, cache_b_v, c_prompt, c_sample, w_ada, b_ada, w_in, sinks_b, gain_a, gain_b, w_out, ln1_g, ln1_b, ln2_g, ln2_b, router_w, router_b, w_gate, w_up, w_down):
    nbatch, seq, _ = x_prompt.shape
    nsamp = x_sample.shape[0]
    t_p = nbatch * seq
    slopes = jnp.asarray(2.0 ** (-8.0 * np.arange(1, N_HEADS + 1) / N_HEADS), dtype=F32)
    slopes_col = slopes.reshape(N_HEADS, 1)

    perm = np.array([g * EPG + j for j in range(EPG) for g in range(N_GROUPS)])
    rw_t = router_w.T[perm]
    rb = router_b[perm].reshape(N_EXPERTS, 1)

    pad = (-(nbatch + nsamp)) % SUBLANES
    c_all = jnp.concatenate([c_prompt, c_sample, jnp.zeros((pad, D_MODEL), F32)], axis=0)
    mod_all = _ada(c_all, w_ada, b_ada)

    to_lanes = lambda c: jnp.transpose(c, (0, 1, 3, 4, 2))
    from_lanes = lambda c: jnp.transpose(c, (0, 1, 4, 2, 3))
    caches = [to_lanes(c) for c in (cache_a_k, cache_a_v, cache_b_k, cache_b_v)]
    new_caches = None
    cols = lambda x, hh: jnp.transpose(x.reshape(nsamp, hh, HEAD_DIM), (0, 2, 1))

    xp = x_prompt.reshape(t_p, D_MODEL)
    xs = x_sample.reshape(nsamp, D_MODEL)
    pak, pav, pbk, pbv = [], [], [], []
    kb0 = 4 * A_WIDTH
    for l in range(DEPTH):
        mod_p = mod_all[l, :nbatch].reshape(nbatch * 6, 1, D_MODEL)
        mod_s = mod_all[l, nbatch:nbatch + nsamp].reshape(nsamp, 6, D_MODEL).transpose(1, 0, 2)
        ga, gb = gain_a[l].reshape(1, A_WIDTH), gain_b[l].reshape(1, A_WIDTH)
        l1g, l1b = ln1_g[l].reshape(1, D_MODEL), ln1_b[l].reshape(1, D_MODEL)
        l2g, l2b = ln2_g[l].reshape(1, D_MODEL), ln2_b[l].reshape(1, D_MODEL)

        h = _proj(xp, mod_p, w_in[l], P_TR, seq // P_TR)
        oa = _attn_a(h, slopes, nbatch, seq)
        ob = _attn_b(h, slopes, sinks_b[l], nbatch, seq)
        x1, u2c, gid, cnt = _merge(oa, ob, xp, mod_p, ga, gb, w_out[l], l1g, l1b, rw_t, rb, P_TR2, seq // P_TR2)
        xp = _moe_layer(u2c, gid, cnt, x1, mod_p, w_gate, w_up, w_down, l2g, l2b, l, P_TR2, seq // P_TR2,
                        P_TM, P_CHUNK)
        h3 = h.reshape(nbatch, seq, IN_COLS)
        pak.append(h3[:, seq - WIN_A:, A_WIDTH:2 * A_WIDTH].reshape(nbatch, WIN_A, N_HEADS, HEAD_DIM))
        pav.append(h3[:, seq - WIN_A:, 2 * A_WIDTH:3 * A_WIDTH].reshape(nbatch, WIN_A, N_HEADS, HEAD_DIM))
        pbk.append(h3[:, seq - WIN_B:, kb0:kb0 + LANES].reshape(nbatch, WIN_B, KV_B, HEAD_DIM))
        pbv.append(h3[:, seq - WIN_B:, kb0 + LANES:kb0 + 2 * LANES].reshape(nbatch, WIN_B, KV_B, HEAD_DIM))

        hs = _proj(xs, mod_s, w_in[l], nsamp, 1)
        oas, obs, *new_caches = _sample_attn(
            l, slopes_col, sinks_b[l].reshape(N_HEADS, 1),
            cols(hs[:, 0:A_WIDTH], N_HEADS), cols(hs[:, A_WIDTH:2 * A_WIDTH], N_HEADS),
            cols(hs[:, 2 * A_WIDTH:3 * A_WIDTH], N_HEADS), cols(hs[:, 3 * A_WIDTH:4 * A_WIDTH], N_HEADS),
            cols(hs[:, kb0:kb0 + LANES], KV_B), cols(hs[:, kb0 + LANES:kb0 + 2 * LANES], KV_B),
            caches, new_caches)
        oas = jnp.transpose(oas, (0, 2, 1)).reshape(nsamp, A_WIDTH)
        obs = jnp.transpose(obs, (0, 2, 1)).reshape(nsamp, A_WIDTH)
        x1s, u2cs, gids, cnts = _merge(oas, obs, xs, mod_s, ga, gb, w_out[l], l1g, l1b, rw_t, rb, nsamp, 1)
        xs = _moe_layer(u2cs, gids, cnts, x1s, mod_s, w_gate, w_up, w_down, l2g, l2b, l, nsamp, 1, nsamp, nsamp)

    sak, sav, sbk, sbv = [from_lanes(c) for c in new_caches]
    return (xp.reshape(nbatch, seq, D_MODEL), xs.reshape(nsamp, 1, D_MODEL),
            jnp.stack(pak), jnp.stack(pav), jnp.stack(pbk), jnp.stack(pbv), sak, sav, sbk, sbv)
```

```python
import functools

import jax
import jax.numpy as jnp
import numpy as np
from jax import lax
from jax.experimental import pallas as pl
from jax.experimental.pallas import tpu as pltpu

F32 = jnp.float32
BF16 = jnp.bfloat16
I32 = jnp.int32

D_MODEL = 1024
DEPTH = 4
HEAD_DIM = 64
N_HEADS = 8
KV_B = 2
G_B = N_HEADS // KV_B
DILATIONS = ((128, 1), (512, 4), (2048, 16))
BAND = 128
WIN_A = 2048
WIN_B = 128
N_EXPERTS = 32
N_GROUPS = 8
EPG = N_EXPERTS // N_GROUPS
D_EXPERT = D_MODEL // 4
ALPHA = (2.0 * DEPTH) ** 0.25
LN_EPS = 1e-5
NEG = -1e30
ATTN_SCALE = HEAD_DIM ** -0.5
A_WIDTH = N_HEADS * HEAD_DIM
IN_COLS = 3 * A_WIDTH + A_WIDTH + 2 * KV_B * HEAD_DIM

LANES = 128
SUBLANES = 8
SUP = BAND * 16
Q4 = SUP // 4
UNROLL = 8
TAIL = LANES
ROW_W = D_MODEL + TAIL


def _silu(x):
    return x * jax.nn.sigmoid(x)


def _layer_norm(z, g, b):
    mu = jnp.mean(z, axis=-1, keepdims=True)
    zc = z - mu
    var = jnp.mean(zc * zc, axis=-1, keepdims=True)
    return zc * lax.rsqrt(var + LN_EPS) * g + b


def _rms_norm(x, g):
    return x * lax.rsqrt(jnp.mean(x * x, axis=-1, keepdims=True) + LN_EPS) * g


def _arb(n):
    return pltpu.CompilerParams(dimension_semantics=("arbitrary",) * n)


ADA_TN = 512


def _ada_body(c_ref, w_ref, b_ref, o_ref):
    a = _silu(c_ref[...]).astype(BF16)
    o_ref[...] = jnp.dot(a, w_ref[...].astype(BF16), preferred_element_type=F32) + b_ref[...]


def _ada(c_all, w_ada, b_ada):
    rows = c_all.shape[0]
    ncol = w_ada.shape[2]
    return pl.pallas_call(
        _ada_body,
        grid=(DEPTH, ncol // ADA_TN),
        in_specs=[
            pl.BlockSpec((rows, D_MODEL), lambda l, j: (0, 0)),
            pl.BlockSpec((None, D_MODEL, ADA_TN), lambda l, j: (l, 0, j)),
            pl.BlockSpec((None, 1, ADA_TN), lambda l, j: (l, 0, j)),
        ],
        out_specs=pl.BlockSpec((None, rows, ADA_TN), lambda l, j: (l, 0, j)),
        out_shape=jax.ShapeDtypeStruct((DEPTH, rows, ncol), F32),
        compiler_params=_arb(2),
        name="ada",
    )(c_all, w_ada, b_ada.reshape(DEPTH, 1, ncol))


def _mod_spec(k, tiles_per_batch, r):
    return pl.BlockSpec((None, r, D_MODEL), lambda i: ((i // tiles_per_batch) * 6 + k, 0, 0))


def _proj_body(x_ref, sh_ref, sc_ref, w_ref, o_ref, wb):
    @pl.when(pl.program_id(0) == 0)
    def _():
        wb[...] = w_ref[...].astype(BF16)

    u = x_ref[...] * (1.0 + sc_ref[...]) + sh_ref[...]
    o_ref[...] = jnp.dot(u.astype(BF16), wb[...], preferred_element_type=F32)


def _proj(x, mod, w_in_l, tr, tiles_per_batch):
    t = x.shape[0]
    r = mod.shape[1]
    return pl.pallas_call(
        _proj_body,
        grid=(t // tr,),
        in_specs=[
            pl.BlockSpec((tr, D_MODEL), lambda i: (i, 0)),
            _mod_spec(0, tiles_per_batch, r),
            _mod_spec(1, tiles_per_batch, r),
            pl.BlockSpec((D_MODEL, IN_COLS), lambda i: (0, 0)),
        ],
        out_specs=pl.BlockSpec((tr, IN_COLS), lambda i: (i, 0)),
        out_shape=jax.ShapeDtypeStruct((t, IN_COLS), F32),
        scratch_shapes=[pltpu.VMEM((D_MODEL, IN_COLS), BF16)],
        compiler_params=_arb(1),
        name="proj",
    )(x, mod, mod, w_in_l)


def _band_unit(q, kw, vw, bias):
    lane = lax.broadcasted_iota(I32, (BAND, LANES), 1)
    qs = q * ATTN_SCALE
    q2 = jnp.concatenate([jnp.where(lane < HEAD_DIM, qs, 0.0), jnp.where(lane >= HEAD_DIM, qs, 0.0)], axis=0)
    s = lax.dot_general(q2.astype(BF16), kw.astype(BF16), (((1,), (1,)), ((), ())), preferred_element_type=F32)
    s = s + bias
    m = jnp.max(s, axis=1, keepdims=True)
    pe = jnp.exp(s - m).astype(BF16)
    vx = jnp.concatenate([vw.astype(BF16), jnp.ones((2 * BAND, LANES), BF16)], axis=1)
    pvx = jnp.dot(pe, vx, preferred_element_type=F32)
    return pvx[:, :LANES], m, pvx[:, LANES:]


def _pair_merge(a):
    lane = lax.broadcasted_iota(I32, (BAND, LANES), 1)
    return jnp.where(lane < HEAD_DIM, jnp.broadcast_to(a[:BAND], (BAND, LANES)),
                     jnp.broadcast_to(a[BAND:], (BAND, LANES)))


def _make_bias(slope0, slope1, d, first):
    row = lax.broadcasted_iota(I32, (2 * BAND, 2 * BAND), 0)
    col = lax.broadcasted_iota(I32, (2 * BAND, 2 * BAND), 1)
    i = jnp.where(row >= BAND, row - BAND, row)
    dist = i + BAND - col
    valid = (dist >= 0) & (dist <= BAND)
    if first:
        valid = valid & (col >= BAND)
    slope = jnp.where(row >= BAND, slope1, slope0)
    return jnp.where(valid, -(slope * (dist * d).astype(F32)), NEG)


def _attn_a_body(slopes_ref, q_ref, k_ref, v_ref, o_ref, qa, ka, va, ktail, vtail, acc, ms, ls, bias):
    p = pl.program_id(1)
    st = pl.program_id(2)
    cur = st % 2
    prv = 1 - cur

    @pl.when(st == 0)
    def _():
        ka[1] = jnp.zeros((4, Q4, LANES), F32)
        va[1] = jnp.zeros((4, Q4, LANES), F32)
        ktail[...] = jnp.zeros((BAND, LANES), F32)
        vtail[...] = jnp.zeros((BAND, LANES), F32)
        s0 = slopes_ref[2 * p]
        s1 = slopes_ref[2 * p + 1]
        for ci, (_, d) in enumerate(DILATIONS):
            bias[ci, 0] = _make_bias(s0, s1, d, False)
            bias[ci, 1] = _make_bias(s0, s1, d, True)

    for c in range(4):
        qa[c] = q_ref[pl.ds(c, Q4, stride=4), :]
        ka[cur, c] = k_ref[pl.ds(c, Q4, stride=4), :]
        va[cur, c] = v_ref[pl.ds(c, Q4, stride=4), :]

    first0 = (st == 0).astype(I32)
    n_groups = SUP // BAND // UNROLL

    def store(ci, idx, pv, m, l):
        acc[ci, idx, :] = _pair_merge(pv)
        ms[ci, idx, :] = _pair_merge(m)
        ls[ci, idx, :] = _pair_merge(l)

    pv, m, l = _band_unit(q_ref[0:BAND, :], jnp.concatenate([ktail[...], k_ref[0:BAND, :]], axis=0),
                          jnp.concatenate([vtail[...], v_ref[0:BAND, :]], axis=0), bias[0, first0])
    store(0, pl.ds(0, BAND), pv, m, l)

    def unit1(n):
        qs = pl.multiple_of(n * BAND, BAND)
        ks = pl.multiple_of(n * BAND - BAND, BAND)
        pv, m, l = _band_unit(q_ref[pl.ds(qs, BAND), :], k_ref[pl.ds(ks, 2 * BAND), :],
                              v_ref[pl.ds(ks, 2 * BAND), :], bias[0, 0])
        store(0, pl.ds(qs, BAND), pv, m, l)

    for n in range(1, UNROLL):
        unit1(n)

    def group1(g, carry):
        for j in range(UNROLL):
            unit1(g * UNROLL + j)
        return carry
    lax.fori_loop(1, n_groups, group1, 0)
    ktail[...] = k_ref[SUP - BAND:SUP, :]
    vtail[...] = v_ref[SUP - BAND:SUP, :]

    def unit4(u, r, first):
        qs = pl.multiple_of(u * BAND, BAND)
        if first:
            kw = jnp.concatenate([ka[prv, r, Q4 - BAND:Q4, :], ka[cur, r, 0:BAND, :]], axis=0)
            vw = jnp.concatenate([va[prv, r, Q4 - BAND:Q4, :], va[cur, r, 0:BAND, :]], axis=0)
            b = bias[1, first0]
        else:
            kw = ka[cur, r, pl.ds(qs - BAND, 2 * BAND), :]
            vw = va[cur, r, pl.ds(qs - BAND, 2 * BAND), :]
            b = bias[1, 0]
        pv, m, l = _band_unit(qa[r, pl.ds(qs, BAND), :], kw, vw, b)
        store(1, pl.ds(r * Q4 + qs, BAND), pv, m, l)

    for n in range(UNROLL):
        unit4(n // 4, n % 4, n < 4)

    def group4(g, carry):
        for j in range(UNROLL):
            unit4(g * (UNROLL // 4) + j // 4, j % 4, False)
        return carry
    lax.fori_loop(1, n_groups, group4, 0)

    def group16(g, carry):
        for j in range(UNROLL):
            c = g * (UNROLL // 4) + j // 4
            a = j % 4
            idx = pl.ds(a, BAND, stride=4)
            kw = jnp.concatenate([ka[prv, c, idx, :], ka[cur, c, idx, :]], axis=0)
            vw = jnp.concatenate([va[prv, c, idx, :], va[cur, c, idx, :]], axis=0)
            pv, m, l = _band_unit(qa[c, idx, :], kw, vw, bias[2, first0])
            store(2, pl.ds(c * Q4 + a, BAND, stride=4), pv, m, l)
        return carry
    lax.fori_loop(0, n_groups, group16, 0)

    def fin(i, carry):
        c = i // 4
        rows = pl.ds(pl.multiple_of(i * BAND, BAND), BAND)
        tok = pl.ds(c + 4 * BAND * (i - 4 * c), BAND, stride=4)
        m0, m1, m2 = ms[0, tok, :], ms[1, rows, :], ms[2, rows, :]
        mm = jnp.maximum(jnp.maximum(m0, m1), m2)
        w0, w1, w2 = jnp.exp(m0 - mm), jnp.exp(m1 - mm), jnp.exp(m2 - mm)
        den = w0 * ls[0, tok, :] + w1 * ls[1, rows, :] + w2 * ls[2, rows, :]
        num = w0 * acc[0, tok, :] + w1 * acc[1, rows, :] + w2 * acc[2, rows, :]
        o_ref[tok, :] = num / den
        return carry
    lax.fori_loop(0, SUP // BAND, fin, 0)


def _attn_a(h, slopes, nbatch, seq):
    assert UNROLL % 4 == 0 and (SUP // BAND) % UNROLL == 0 and seq % SUP == 0
    nst = seq // SUP
    npair = N_HEADS // 2

    def col(c0):
        return lambda b, p, st: (b * nst + st, c0 + p)

    return pl.pallas_call(
        _attn_a_body,
        grid=(nbatch, npair, nst),
        in_specs=[
            pl.BlockSpec(memory_space=pltpu.SMEM),
            pl.BlockSpec((SUP, LANES), col(0)),
            pl.BlockSpec((SUP, LANES), col(npair)),
            pl.BlockSpec((SUP, LANES), col(2 * npair)),
        ],
        out_specs=pl.BlockSpec((SUP, LANES), col(0)),
        out_shape=jax.ShapeDtypeStruct((nbatch * seq, A_WIDTH), F32),
        scratch_shapes=[
            pltpu.VMEM((4, Q4, LANES), F32),
            pltpu.VMEM((2, 4, Q4, LANES), F32),
            pltpu.VMEM((2, 4, Q4, LANES), F32),
            pltpu.VMEM((BAND, LANES), F32),
            pltpu.VMEM((BAND, LANES), F32),
            pltpu.VMEM((3, SUP, LANES), F32),
            pltpu.VMEM((3, SUP, LANES), F32),
            pltpu.VMEM((3, SUP, LANES), F32),
            pltpu.VMEM((3, 2, 2 * BAND, 2 * BAND), F32),
        ],
        compiler_params=_arb(3),
        name="attn_a",
    )(slopes, h, h, h)


def _attn_b_body(slopes_ref, sinks_ref, q_ref, k_ref, v_ref, o_ref, k2, v2, bias):
    p = pl.program_id(1)
    st = pl.program_id(2)

    @pl.when(st == 0)
    def _():
        k2[0:BAND, :] = jnp.zeros((BAND, LANES), F32)
        v2[0:BAND, :] = jnp.zeros((BAND, LANES), F32)
        s0 = slopes_ref[2 * p]
        s1 = slopes_ref[2 * p + 1]
        bias[0] = _make_bias(s0, s1, 1, False)
        bias[1] = _make_bias(s0, s1, 1, True)

    @pl.when(st > 0)
    def _():
        k2[0:BAND, :] = k2[SUP:SUP + BAND, :]
        v2[0:BAND, :] = v2[SUP:SUP + BAND, :]

    lane = lax.broadcasted_iota(I32, (SUP, LANES), 1)
    keep = (lane < HEAD_DIM) == (p < G_B // 2)
    k2[BAND:BAND + SUP, :] = jnp.where(keep, k_ref[...], pltpu.roll(k_ref[...], HEAD_DIM, 1))
    v2[BAND:BAND + SUP, :] = jnp.where(keep, v_ref[...], pltpu.roll(v_ref[...], HEAD_DIM, 1))

    row = lax.broadcasted_iota(I32, (2 * BAND, 1), 0)
    sink = jnp.where(row >= BAND, sinks_ref[2 * p + 1], sinks_ref[2 * p])

    def group(g, carry):
        for j in range(UNROLL):
            n = g * UNROLL + j
            qstart = pl.multiple_of(n * BAND, BAND)
            first = jnp.logical_and(st == 0, n == 0).astype(I32)
            pv, m, l = _band_unit(q_ref[pl.ds(qstart, BAND), :], k2[pl.ds(qstart, 2 * BAND), :],
                                  v2[pl.ds(qstart, 2 * BAND), :], bias[first])
            m2 = jnp.maximum(m, sink)
            scale = jnp.exp(m - m2)
            den = l * scale + jnp.exp(sink - m2)
            o_ref[pl.ds(qstart, BAND), :] = _pair_merge(pv * scale / den)
        return carry
    lax.fori_loop(0, SUP // BAND // UNROLL, group, 0)


def _attn_b(h, slopes, sinks_l, nbatch, seq):
    nst = seq // SUP
    npair = N_HEADS // 2
    qcol0 = 3 * npair
    kcol = 4 * npair
    vcol = kcol + 1
    return pl.pallas_call(
        _attn_b_body,
        grid=(nbatch, npair, nst),
        in_specs=[
            pl.BlockSpec(memory_space=pltpu.SMEM),
            pl.BlockSpec(memory_space=pltpu.SMEM),
            pl.BlockSpec((SUP, LANES), lambda b, p, st: (b * nst + st, qcol0 + p)),
            pl.BlockSpec((SUP, LANES), lambda b, p, st: (b * nst + st, kcol)),
            pl.BlockSpec((SUP, LANES), lambda b, p, st: (b * nst + st, vcol)),
        ],
        out_specs=pl.BlockSpec((SUP, LANES), lambda b, p, st: (b * nst + st, p)),
        out_shape=jax.ShapeDtypeStruct((nbatch * seq, A_WIDTH), F32),
        scratch_shapes=[
            pltpu.VMEM((BAND + SUP, LANES), F32),
            pltpu.VMEM((BAND + SUP, LANES), F32),
            pltpu.VMEM((2, 2 * BAND, 2 * BAND), F32),
        ],
        compiler_params=_arb(3),
        name="attn_b",
    )(slopes, sinks_l, h, h, h)


def _route(logits_t, rb):
    aff = jax.nn.sigmoid(logits_t)
    sel = aff + rb
    a = [aff[SUBLANES * j:SUBLANES * (j + 1)] for j in range(EPG)]
    s = [sel[SUBLANES * j:SUBLANES * (j + 1)] for j in range(EPG)]
    hi1, lo1 = jnp.maximum(s[0], s[1]), jnp.minimum(s[0], s[1])
    hi2, lo2 = jnp.maximum(s[2], s[3]), jnp.minimum(s[2], s[3])
    top1 = jnp.maximum(hi1, hi2)
    top2 = jnp.maximum(jnp.minimum(hi1, hi2), jnp.maximum(lo1, lo2))
    gscore = top1 + top2
    giota = lax.broadcasted_iota(I32, gscore.shape, 0)
    gmax = jnp.max(gscore, axis=0, keepdims=True)
    gbest = jnp.min(jnp.where(gscore == gmax, giota, N_GROUPS), axis=0, keepdims=True)
    gmask = giota == gbest
    v = [jnp.sum(jnp.where(gmask, sj, 0.0), axis=0, keepdims=True) for sj in s]
    av = [jnp.sum(jnp.where(gmask, aj, 0.0), axis=0, keepdims=True) for aj in a]
    m1 = jnp.maximum(jnp.maximum(v[0], v[1]), jnp.maximum(v[2], v[3]))
    i1 = jnp.where(v[0] == m1, 0, jnp.where(v[1] == m1, 1, jnp.where(v[2] == m1, 2, 3)))
    w = [jnp.where(i1 == j, -jnp.inf, v[j]) for j in range(EPG)]
    m2 = jnp.maximum(jnp.maximum(w[0], w[1]), jnp.maximum(w[2], w[3]))
    i2 = jnp.where(w[0] == m2, 0, jnp.where(w[1] == m2, 1, jnp.where(w[2] == m2, 2, 3)))
    a1 = sum(jnp.where(i1 == j, av[j], 0.0) for j in range(EPG))
    a2 = sum(jnp.where(i2 == j, av[j], 0.0) for j in range(EPG))
    den = a1 + a2
    comb = [jnp.where(i1 == j, a1 / den, 0.0) + jnp.where(i2 == j, a2 / den, 0.0) for j in range(EPG)]
    return gbest, comb


RUN_ALIGN = SUBLANES
LP_LANE = EPG


def _round_up_i32(x, m):
    shift = int(np.log2(m))
    return ((x + (m - 1)) >> shift) << shift


def _group_starts(padded):
    giota = lax.broadcasted_iota(I32, padded.shape, 0)
    off = jnp.zeros(padded.shape, I32)
    for g in range(N_GROUPS - 1):
        off = off + jnp.where(giota > g, padded[g:g + 1, :], 0)
    return off


def _merge_body(oa_ref, ob_ref, x_ref, gt_ref, sh_ref, sc_ref, ga_ref, gb_ref, w_ref, lg_ref, lb_ref,
                rw_ref, rb_ref, x1_ref, u2_ref, tail_ref, lp_ref, cnt_ref, wb, tri, *, sbk):
    @pl.when(pl.program_id(0) == 0)
    def _():
        wb[...] = w_ref[...].astype(BF16)
        r_i = lax.broadcasted_iota(I32, (sbk, sbk), 0)
        c_i = lax.broadcasted_iota(I32, (sbk, sbk), 1)
        tri[...] = (r_i <= c_i).astype(BF16)

    def rows_of(ref, rows):
        return ref[...] if ref.shape[0] == 1 else ref[rows, :]

    for k in range(x_ref.shape[0] // sbk):
        rows = slice(k * sbk, (k + 1) * sbk)
        ya = _rms_norm(oa_ref[rows, :], ga_ref[...])
        yb = _rms_norm(ob_ref[rows, :], gb_ref[...])
        y = jnp.concatenate([ya, yb], axis=-1).astype(BF16)
        mix = jnp.dot(y, wb[...], preferred_element_type=F32)
        x1 = _layer_norm(ALPHA * x_ref[rows, :] + rows_of(gt_ref, rows) * mix, lg_ref[...], lb_ref[...])
        x1_ref[rows, :] = x1
        u2 = x1 * (1.0 + rows_of(sc_ref, rows)) + rows_of(sh_ref, rows)
        u2_ref[rows, :] = u2

        logits_t = lax.dot_general(rw_ref[...], u2, (((1,), (1,)), ((), ())), precision=lax.Precision.HIGHEST,
                                   preferred_element_type=F32)
        gbest, comb = _route(logits_t, rb_ref[...])
        onehot = lax.broadcasted_iota(I32, (N_GROUPS, sbk), 0) == gbest
        cnt = jnp.sum(onehot.astype(F32), axis=1, keepdims=True)
        starts = _group_starts(_round_up_i32(cnt.astype(I32), RUN_ALIGN)).astype(F32)
        inc = jnp.dot(onehot.astype(BF16), tri[...], preferred_element_type=F32)
        lp = jnp.sum(jnp.where(onehot, starts + inc - 1.0, 0.0), axis=0, keepdims=True)
        lp_ref[k] = lp
        cnt_ref[k] = jnp.broadcast_to(cnt, (N_GROUPS, LANES))
        row = lax.broadcasted_iota(I32, (TAIL, sbk), 0)
        slab = jnp.where(row == LP_LANE, lp, 0.0)
        for j in range(EPG):
            slab = jnp.where(row == j, comb[j], slab)
        tail_ref[rows, :] = slab.T


def _merge(oa, ob, x, mod, ga, gb, w_out_l, lg, lb, rw_t, rb, tr, tiles_per_batch, sbk):
    t = x.shape[0]
    r = mod.shape[1]
    nt = t // tr
    nsb = tr // sbk
    row = lambda i: (i, 0)
    fixed = lambda i: (0, 0)
    return pl.pallas_call(
        functools.partial(_merge_body, sbk=sbk),
        grid=(nt,),
        in_specs=[
            pl.BlockSpec((tr, A_WIDTH), row),
            pl.BlockSpec((tr, A_WIDTH), row),
            pl.BlockSpec((tr, D_MODEL), row),
            _mod_spec(2, tiles_per_batch, r),
            _mod_spec(3, tiles_per_batch, r),
            _mod_spec(4, tiles_per_batch, r),
            pl.BlockSpec((1, A_WIDTH), fixed),
            pl.BlockSpec((1, A_WIDTH), fixed),
            pl.BlockSpec((D_MODEL, D_MODEL), fixed),
            pl.BlockSpec((1, D_MODEL), fixed),
            pl.BlockSpec((1, D_MODEL), fixed),
            pl.BlockSpec((N_EXPERTS, D_MODEL), fixed),
            pl.BlockSpec((N_EXPERTS, 1), fixed),
        ],
        out_specs=[
            pl.BlockSpec((tr, D_MODEL), row),
            pl.BlockSpec((tr, D_MODEL), row),
            pl.BlockSpec((tr, TAIL), row),
            pl.BlockSpec((nsb, 1, sbk), lambda i: (i, 0, 0)),
            pl.BlockSpec((nsb, N_GROUPS, LANES), lambda i: (i, 0, 0)),
        ],
        out_shape=[
            jax.ShapeDtypeStruct((t, D_MODEL), F32),
            jax.ShapeDtypeStruct((t, D_MODEL), F32),
            jax.ShapeDtypeStruct((t, TAIL), F32),
            jax.ShapeDtypeStruct((t // sbk, 1, sbk), F32),
            jax.ShapeDtypeStruct((t // sbk, N_GROUPS, LANES), F32),
        ],
        scratch_shapes=[pltpu.VMEM((D_MODEL, D_MODEL), BF16), pltpu.VMEM((sbk, sbk), BF16)],
        compiler_params=_arb(1),
        name="merge",
    )(oa, ob, x, mod, mod, mod, ga, gb, w_out_l, lg, lb, rw_t, rb)


PLAN_SIZE, PLAN_LOCAL, PLAN_GLOBAL = 0, N_GROUPS, 2 * N_GROUPS


def _plan_body(cnt_ref, plan_ref, meta_ref, run_off, *, tm):
    nt = cnt_ref.shape[0]

    def sizes(i):
        return _round_up_i32(cnt_ref[i].astype(I32), RUN_ALIGN)

    def first(i, total):
        run_off[i] = total
        return total + sizes(i)
    totals = lax.fori_loop(0, nt, first, jnp.zeros((N_GROUPS, LANES), I32))
    padded = _round_up_i32(totals, tm)
    goff = _group_starts(padded)
    ends = goff + padded
    giota = lax.broadcasted_iota(I32, (N_GROUPS, LANES), 0)
    lane = lax.broadcasted_iota(I32, (N_GROUPS, LANES), 1)
    tile_group = jnp.sum((ends <= lane * tm).astype(F32), axis=0, keepdims=True).astype(I32)
    tile_group = jnp.minimum(tile_group, N_GROUPS - 1)
    n_used = ends[N_GROUPS - 1:N_GROUPS, :] >> int(np.log2(tm))
    meta_ref[...] = jnp.where(giota == 0, tile_group, jnp.where(giota == 1, n_used, 0))

    def on_lanes(v):
        picked = jnp.where(giota == (lane & (N_GROUPS - 1)), v, 0).astype(F32)
        return jnp.sum(picked, axis=0, keepdims=True).astype(I32)

    lane1 = lax.broadcasted_iota(I32, (1, LANES), 1)

    def second(i, carry):
        sz = sizes(i)
        row = jnp.where(lane1 < PLAN_LOCAL, on_lanes(sz),
                        jnp.where(lane1 < PLAN_GLOBAL, on_lanes(_group_starts(sz)), on_lanes(goff + run_off[i])))
        plan_ref[i] = row
        return carry
    lax.fori_loop(0, nt, second, 0)


def _plan(cnt, tm):
    nt = cnt.shape[0]
    return pl.pallas_call(
        functools.partial(_plan_body, tm=tm),
        out_shape=[jax.ShapeDtypeStruct((nt, 1, LANES), I32), jax.ShapeDtypeStruct((N_GROUPS, LANES), I32)],
        scratch_shapes=[pltpu.VMEM((nt, N_GROUPS, LANES), I32)],
        name="plan",
    )(cnt)


def _run_copies(plan_ref, sbk, make):
    out = []
    sizes = [s for s in (256, 128, 64, 32, 16, 8) if s <= sbk]
    for g in range(N_GROUPS):
        n = plan_ref[0, PLAN_SIZE + g]
        loc = plan_ref[0, PLAN_LOCAL + g]
        glo = plan_ref[0, PLAN_GLOBAL + g]
        done = jnp.int32(0)
        for s in sizes:
            cond = (n & s) != 0
            out.append((cond, make(pl.multiple_of(loc + done, RUN_ALIGN), pl.multiple_of(glo + done, RUN_ALIGN), s)))
            done = done + jnp.where(cond, s, 0)
    return out


def _start_then_wait(copies):
    for cond, cp in copies:
        pl.when(cond)(cp.start)
    for cond, cp in copies:
        pl.when(cond)(cp.wait)


def _split_bf16(x):
    hi = x.astype(BF16)
    return hi, (x - hi.astype(F32)).astype(BF16)


def _scatter_body(plan_ref, lp_ref, u_ref, tail_ref, dst_in, dst_ref, buf, sem):
    del dst_in
    sbk = u_ref.shape[0]
    lr = buf.shape[0]
    perm = (lax.broadcasted_iota(I32, (lr, sbk), 0) == lp_ref[...].astype(I32)).astype(BF16)
    t_hi, t_lo = _split_bf16(tail_ref[...])
    x = jnp.concatenate([u_ref[...].astype(BF16), t_hi, t_lo], axis=1)
    y = jnp.dot(perm, x, preferred_element_type=F32)
    buf[:, 0:D_MODEL] = y[:, 0:D_MODEL]
    buf[:, D_MODEL:ROW_W] = y[:, D_MODEL:ROW_W] + y[:, ROW_W:ROW_W + TAIL]
    _start_then_wait(_run_copies(plan_ref, sbk, lambda loc, glo, s: pltpu.make_async_copy(
        buf.at[pl.ds(loc, s), :], dst_ref.at[pl.ds(glo, s), :], sem)))


def _scatter_rows(plan, lp, u2, tail, n_sorted, sbk):
    t = u2.shape[0]
    dst = jnp.zeros((n_sorted, ROW_W), F32)
    return pl.pallas_call(
        _scatter_body,
        grid=(t // sbk,),
        in_specs=[
            pl.BlockSpec((None, 1, LANES), lambda i: (i, 0, 0), memory_space=pltpu.SMEM),
            pl.BlockSpec((None, 1, sbk), lambda i: (i, 0, 0)),
            pl.BlockSpec((sbk, D_MODEL), lambda i: (i, 0)),
            pl.BlockSpec((sbk, TAIL), lambda i: (i, 0)),
            pl.BlockSpec(memory_space=pl.ANY),
        ],
        out_specs=pl.BlockSpec(memory_space=pl.ANY),
        out_shape=jax.ShapeDtypeStruct((n_sorted, ROW_W), F32),
        scratch_shapes=[pltpu.VMEM((sbk + LANES, ROW_W), F32), pltpu.SemaphoreType.DMA],
        input_output_aliases={4: 0},
        compiler_params=_arb(1),
        name="scatter_rows",
    )(plan, lp, u2, tail, dst)


def _combine_body(plan_ref, ys_ref, tail_ref, x1_ref, gt_ref, lg_ref, lb_ref, o_ref, buf, sem):
    sbk = x1_ref.shape[0]
    lr = buf.shape[0]
    _start_then_wait(_run_copies(plan_ref, sbk, lambda loc, glo, s: pltpu.make_async_copy(
        ys_ref.at[pl.ds(glo, s), :], buf.at[pl.ds(loc, s), :], sem)))
    n_rows = plan_ref[0, PLAN_LOCAL + N_GROUPS - 1] + plan_ref[0, PLAN_SIZE + N_GROUPS - 1]
    ys = jnp.where(lax.broadcasted_iota(I32, (lr, 1), 0) < n_rows, buf[...], 0.0)
    lp = tail_ref[:, LP_LANE:LP_LANE + 1].astype(I32)
    perm_t = (lax.broadcasted_iota(I32, (sbk, lr), 1) == lp).astype(BF16)
    y_hi, y_lo = _split_bf16(ys)
    y = jnp.dot(perm_t, y_hi, preferred_element_type=F32) + jnp.dot(perm_t, y_lo, preferred_element_type=F32)
    o_ref[...] = _layer_norm(ALPHA * x1_ref[...] + gt_ref[...] * y, lg_ref[...], lb_ref[...])


def _combine(plan, ys, tail, x1, mod, lg, lb, sbk, tiles_per_batch):
    t = x1.shape[0]
    r = mod.shape[1]
    return pl.pallas_call(
        _combine_body,
        grid=(t // sbk,),
        in_specs=[
            pl.BlockSpec((None, 1, LANES), lambda i: (i, 0, 0), memory_space=pltpu.SMEM),
            pl.BlockSpec(memory_space=pl.ANY),
            pl.BlockSpec((sbk, TAIL), lambda i: (i, 0)),
            pl.BlockSpec((sbk, D_MODEL), lambda i: (i, 0)),
            _mod_spec(5, tiles_per_batch, r),
            pl.BlockSpec((1, D_MODEL), lambda i: (0, 0)),
            pl.BlockSpec((1, D_MODEL), lambda i: (0, 0)),
        ],
        out_specs=pl.BlockSpec((sbk, D_MODEL), lambda i: (i, 0)),
        out_shape=jax.ShapeDtypeStruct((t, D_MODEL), F32),
        scratch_shapes=[pltpu.VMEM((sbk + LANES, D_MODEL), F32), pltpu.SemaphoreType.DMA],
        compiler_params=_arb(1),
        name="combine",
    )(plan, ys, tail, x1, mod, lg, lb)


def _moe_body(tg_ref, nu_ref, xs_ref, wg_ref, wu_ref, wd_ref, ys_ref, wgb, wub, wdb):
    i = pl.program_id(0)
    prev = tg_ref[jnp.maximum(i - 1, 0)]

    @pl.when(jnp.logical_or(i == 0, tg_ref[i] != prev))
    def _():
        wgb[...] = wg_ref[...].astype(BF16)
        wub[...] = wu_ref[...].astype(BF16)
        wdb[...] = wd_ref[...].astype(BF16)

    @pl.when(i < nu_ref[0])
    def _():
        x = xs_ref[:, 0:D_MODEL].astype(BF16)
        comb = xs_ref[:, D_MODEL:ROW_W]
        y = jnp.zeros((x.shape[0], D_MODEL), F32)
        for e in range(EPG):
            hg = jnp.dot(x, wgb[e], preferred_element_type=F32)
            hu = jnp.dot(x, wub[e], preferred_element_type=F32)
            act = _silu(hg) * hu * comb[:, e:e + 1]
            y = y + jnp.dot(act.astype(BF16), wdb[e], preferred_element_type=F32)
        ys_ref[...] = y

    @pl.when(i >= nu_ref[0])
    def _():
        ys_ref[...] = jnp.zeros(ys_ref.shape, F32)


def _moe(tile_group, n_used, xs, w_gate, w_up, w_down, layer, tm):
    n_sorted = xs.shape[0]
    nt = n_sorted // tm
    grid_spec = pltpu.PrefetchScalarGridSpec(
        num_scalar_prefetch=2,
        grid=(nt,),
        in_specs=[
            pl.BlockSpec((tm, ROW_W), lambda i, tg, nu: (i, 0)),
            pl.BlockSpec((None, EPG, D_MODEL, D_EXPERT), lambda i, tg, nu: (layer, tg[i], 0, 0)),
            pl.BlockSpec((None, EPG, D_MODEL, D_EXPERT), lambda i, tg, nu: (layer, tg[i], 0, 0)),
            pl.BlockSpec((None, EPG, D_EXPERT, D_MODEL), lambda i, tg, nu: (layer, tg[i], 0, 0)),
        ],
        out_specs=pl.BlockSpec((tm, D_MODEL), lambda i, tg, nu: (i, 0)),
        scratch_shapes=[
            pltpu.VMEM((EPG, D_MODEL, D_EXPERT), BF16),
            pltpu.VMEM((EPG, D_MODEL, D_EXPERT), BF16),
            pltpu.VMEM((EPG, D_EXPERT, D_MODEL), BF16),
        ],
    )
    return pl.pallas_call(
        _moe_body,
        grid_spec=grid_spec,
        out_shape=jax.ShapeDtypeStruct((n_sorted, D_MODEL), F32),
        compiler_params=_arb(1),
        name="moe",
    )(tile_group, n_used, xs, w_gate, w_up, w_down)


def _moe_layer(u2, tail, lp, cnt, x1, mod, w_gate, w_up, w_down, lg, lb, layer, sbk, tiles_per_batch, tm):
    t = x1.shape[0]
    nt = t // sbk
    n_tiles = -(-(t + nt * N_GROUPS * (RUN_ALIGN - 1)) // tm) + N_GROUPS
    assert n_tiles <= LANES
    plan, meta = _plan(cnt, tm)
    xs = _scatter_rows(plan, lp, u2, tail, n_tiles * tm, sbk)
    ys = _moe(meta[0, :n_tiles], meta[1, :1], xs, w_gate, w_up, w_down, layer, tm)
    return _combine(plan, ys, tail, x1, mod, lg, lb, sbk, tiles_per_batch)


def _shift_in(x, new_col):
    n = x.shape[-1]
    lane = lax.broadcasted_iota(I32, x.shape, x.ndim - 1)
    return jnp.where(lane == n - 1, new_col, pltpu.roll(x, n - 1, x.ndim - 1))


def _decode(q_t, k_of, v_of, kn_t, vn_t, head_kv, slope, mult, mult_new, sink):
    n = mult.shape[1]
    rows, rows_new = [], []
    for h in range(N_HEADS):
        g = head_kv[h]
        qh = q_t[:, h:h + 1]
        rows.append(jnp.sum(qh * k_of(g), axis=0, keepdims=True))
        rows_new.append(jnp.sum(qh * kn_t[:, g:g + 1], axis=0, keepdims=True))
    pos = lax.broadcasted_iota(I32, (1, n), 1)
    s = jnp.concatenate(rows, axis=0) * ATTN_SCALE - slope * (n - pos).astype(F32)
    s = jnp.where(mult > 0.0, s, NEG)
    s_new = jnp.concatenate(rows_new, axis=0) * ATTN_SCALE
    m = jnp.maximum(jnp.max(s, axis=1, keepdims=True), s_new)
    if sink is not None:
        m = jnp.maximum(m, sink)
    p = mult * jnp.exp(s - m)
    p_new = mult_new * jnp.exp(s_new - m)
    den = jnp.sum(p, axis=1, keepdims=True) + p_new
    if sink is not None:
        den = den + jnp.exp(sink - m)
    w = p / den
    w_new = p_new / den
    cols = []
    for h in range(N_HEADS):
        g = head_kv[h]
        cols.append(jnp.sum(w[h:h + 1, :] * v_of(g), axis=1, keepdims=True) + w_new[h:h + 1, :] * vn_t[:, g:g + 1])
    return jnp.concatenate(cols, axis=1)


def _dilated_multiplicity(n):
    pos = lax.broadcasted_iota(I32, (1, n), 1)
    dist = n - pos
    mult = jnp.zeros((1, n), F32)
    for (w, d) in DILATIONS:
        mult = mult + jnp.where((dist <= w) & ((dist & (d - 1)) == 0), 1.0, 0.0)
    return mult


N_SAMPLE_IN = 12


def _sample_body(*refs):
    (slope_ref, sink_ref, qa_ref, kna_ref, vna_ref, qb_ref, knb_ref, vnb_ref,
     ka_ref, va_ref, kb_ref, vb_ref) = refs[:N_SAMPLE_IN]
    oa_ref, ob_ref, oka_ref, ova_ref, okb_ref, ovb_ref = refs[-6:]

    slope = slope_ref[...]
    kna, vna = kna_ref[...], vna_ref[...]
    oa_ref[...] = _decode(qa_ref[...], lambda g: ka_ref[g], lambda g: va_ref[g], kna, vna,
                          list(range(N_HEADS)), slope, _dilated_multiplicity(WIN_A), float(len(DILATIONS)), None)
    knb, vnb = knb_ref[...], vnb_ref[...]
    ob_ref[...] = _decode(qb_ref[...], lambda g: kb_ref[g], lambda g: vb_ref[g], knb, vnb,
                          [h // G_B for h in range(N_HEADS)], slope, jnp.ones((1, WIN_B), F32), 1.0, sink_ref[...])

    def shift(h, carry):
        hot = lax.broadcasted_iota(I32, (1, N_HEADS), 1) == h
        oka_ref[h] = _shift_in(ka_ref[h], jnp.sum(jnp.where(hot, kna, 0.0), axis=1, keepdims=True))
        ova_ref[h] = _shift_in(va_ref[h], jnp.sum(jnp.where(hot, vna, 0.0), axis=1, keepdims=True))
        return carry
    lax.fori_loop(0, N_HEADS, shift, 0)
    for g in range(KV_B):
        okb_ref[g] = _shift_in(kb_ref[g], knb[:, g:g + 1])
        ovb_ref[g] = _shift_in(vb_ref[g], vnb[:, g:g + 1])


def _sample_attn(layer, slope, sink_l, qa, kna, vna, qb, knb, vnb, caches, prev_outs):
    nb = qa.shape[0]
    assert caches[0].shape[-1] == WIN_A and caches[2].shape[-1] == WIN_B
    col = lambda hh: pl.BlockSpec((None, HEAD_DIM, hh), lambda i: (i, 0, 0))
    cache = lambda hh, n: pl.BlockSpec((None, None, hh, HEAD_DIM, n), lambda i: (layer, i, 0, 0, 0))
    small = pl.BlockSpec((N_HEADS, 1), lambda i: (0, 0))
    in_specs = [small, small, col(N_HEADS), col(N_HEADS), col(N_HEADS), col(N_HEADS), col(KV_B), col(KV_B),
                cache(N_HEADS, WIN_A), cache(N_HEADS, WIN_A), cache(KV_B, WIN_B), cache(KV_B, WIN_B)]
    args = [slope, sink_l, qa, kna, vna, qb, knb, vnb, *caches]
    assert len(args) == N_SAMPLE_IN
    aliases = {}
    if prev_outs is not None:
        in_specs += [pl.BlockSpec(memory_space=pl.ANY)] * 4
        args += list(prev_outs)
        aliases = {N_SAMPLE_IN + k: 2 + k for k in range(4)}
    return pl.pallas_call(
        _sample_body,
        grid=(nb,),
        in_specs=in_specs,
        out_specs=[col(N_HEADS), col(N_HEADS), cache(N_HEADS, WIN_A), cache(N_HEADS, WIN_A),
                   cache(KV_B, WIN_B), cache(KV_B, WIN_B)],
        out_shape=[jax.ShapeDtypeStruct((nb, HEAD_DIM, N_HEADS), F32)] * 2 + [
            jax.ShapeDtypeStruct(c.shape, F32) for c in caches],
        input_output_aliases=aliases,
        compiler_params=_arb(1),
        name="sample_attn",
    )(*args)


P_TR = 512
P_SBK = 256
P_TM = 256


def kernel(x_prompt, x_sample, cache_a_k, cache_a_v, cache_b_k, cache_b_v, c_prompt, c_sample, w_ada, b_ada, w_in, sinks_b, gain_a, gain_b, w_out, ln1_g, ln1_b, ln2_g, ln2_b, router_w, router_b, w_gate, w_up, w_down):
    nbatch, seq, _ = x_prompt.shape
    nsamp = x_sample.shape[0]
    t_p = nbatch * seq
    slopes = jnp.asarray(2.0 ** (-8.0 * np.arange(1, N_HEADS + 1) / N_HEADS), dtype=F32)
    slopes_col = slopes.reshape(N_HEADS, 1)

    perm = np.array([g * EPG + j for j in range(EPG) for g in range(N_GROUPS)])
    rw_t = router_w.T[perm]
    rb = router_b[perm].reshape(N_EXPERTS, 1)

    pad = (-(nbatch + nsamp)) % SUBLANES
    c_all = jnp.concatenate([c_prompt, c_sample, jnp.zeros((pad, D_MODEL), F32)], axis=0)
    mod_all = _ada(c_all, w_ada, b_ada)

    to_lanes = lambda c: jnp.transpose(c, (0, 1, 3, 4, 2))
    from_lanes = lambda c: jnp.transpose(c, (0, 1, 4, 2, 3))
    caches = [to_lanes(c) for c in (cache_a_k, cache_a_v, cache_b_k, cache_b_v)]
    new_caches = None
    cols = lambda x, hh: jnp.transpose(x.reshape(nsamp, hh, HEAD_DIM), (0, 2, 1))

    xp = x_prompt.reshape(t_p, D_MODEL)
    xs = x_sample.reshape(nsamp, D_MODEL)
    pak, pav, pbk, pbv = [], [], [], []
    kb0 = 4 * A_WIDTH
    for l in range(DEPTH):
        mod_p = mod_all[l, :nbatch].reshape(nbatch * 6, 1, D_MODEL)
        mod_s = mod_all[l, nbatch:nbatch + nsamp].reshape(nsamp, 6, D_MODEL).transpose(1, 0, 2)
        ga, gb = gain_a[l].reshape(1, A_WIDTH), gain_b[l].reshape(1, A_WIDTH)
        l1g, l1b = ln1_g[l].reshape(1, D_MODEL), ln1_b[l].reshape(1, D_MODEL)
        l2g, l2b = ln2_g[l].reshape(1, D_MODEL), ln2_b[l].reshape(1, D_MODEL)

        h = _proj(xp, mod_p, w_in[l], P_TR, seq // P_TR)
        oa = _attn_a(h, slopes, nbatch, seq)
        ob = _attn_b(h, slopes, sinks_b[l], nbatch, seq)
        x1, u2, tail, lp, cnt = _merge(oa, ob, xp, mod_p, ga, gb, w_out[l], l1g, l1b, rw_t, rb, P_TR, seq // P_TR,
                                       P_SBK)
        xp = _moe_layer(u2, tail, lp, cnt, x1, mod_p, w_gate, w_up, w_down, l2g, l2b, l, P_SBK, seq // P_SBK, P_TM)
        h3 = h.reshape(nbatch, seq, IN_COLS)
        pak.append(h3[:, seq - WIN_A:, A_WIDTH:2 * A_WIDTH].reshape(nbatch, WIN_A, N_HEADS, HEAD_DIM))
        pav.append(h3[:, seq - WIN_A:, 2 * A_WIDTH:3 * A_WIDTH].reshape(nbatch, WIN_A, N_HEADS, HEAD_DIM))
        pbk.append(h3[:, seq - WIN_B:, kb0:kb0 + LANES].reshape(nbatch, WIN_B, KV_B, HEAD_DIM))
        pbv.append(h3[:, seq - WIN_B:, kb0 + LANES:kb0 + 2 * LANES].reshape(nbatch, WIN_B, KV_B, HEAD_DIM))

        hs = _proj(xs, mod_s, w_in[l], nsamp, 1)
        oas, obs, *new_caches = _sample_attn(
            l, slopes_col, sinks_b[l].reshape(N_HEADS, 1),
            cols(hs[:, 0:A_WIDTH], N_HEADS), cols(hs[:, A_WIDTH:2 * A_WIDTH], N_HEADS),
            cols(hs[:, 2 * A_WIDTH:3 * A_WIDTH], N_HEADS), cols(hs[:, 3 * A_WIDTH:4 * A_WIDTH], N_HEADS),
            cols(hs[:, kb0:kb0 + LANES], KV_B), cols(hs[:, kb0 + LANES:kb0 + 2 * LANES], KV_B),
            caches, new_caches)
        oas = jnp.transpose(oas, (0, 2, 1)).reshape(nsamp, A_WIDTH)
        obs = jnp.transpose(obs, (0, 2, 1)).reshape(nsamp, A_WIDTH)
        x1s, u2s, tails, lps, cnts = _merge(oas, obs, xs, mod_s, ga, gb, w_out[l], l1g, l1b, rw_t, rb, nsamp, 1,
                                            nsamp)
        xs = _moe_layer(u2s, tails, lps, cnts, x1s, mod_s, w_gate, w_up, w_down, l2g, l2b, l, nsamp, 1, nsamp)

    sak, sav, sbk, sbv = [from_lanes(c) for c in new_caches]
    return (xp.reshape(nbatch, seq, D_MODEL), xs.reshape(nsamp, 1, D_MODEL),
            jnp.stack(pak), jnp.stack(pav), jnp.stack(pbk), jnp.stack(pbv), sak, sav, sbk, sbv)
```

```python
import functools

import jax
import jax.numpy as jnp
import numpy as np
from jax import lax
from jax.experimental import pallas as pl
from jax.experimental.pallas import tpu as pltpu

F32 = jnp.float32
BF16 = jnp.bfloat16
I32 = jnp.int32

D_MODEL = 1024
DEPTH = 4
HEAD_DIM = 64
N_HEADS = 8
KV_B = 2
G_B = N_HEADS // KV_B
DILATIONS = ((128, 1), (512, 4), (2048, 16))
BAND = 128
WIN_A = 2048
WIN_B = 128
N_EXPERTS = 32
N_GROUPS = 8
EPG = N_EXPERTS // N_GROUPS
D_EXPERT = D_MODEL // 4
ALPHA = (2.0 * DEPTH) ** 0.25
LN_EPS = 1e-5
NEG = -1e30
ATTN_SCALE = HEAD_DIM ** -0.5
A_WIDTH = N_HEADS * HEAD_DIM
IN_COLS = 3 * A_WIDTH + A_WIDTH + 2 * KV_B * HEAD_DIM

LANES = 128
SUBLANES = 8
SUP = BAND * 16
Q4 = SUP // 4
UNROLL = 8
TAIL = LANES
ROW_W = D_MODEL + TAIL


def _silu(x):
    return x * jax.nn.sigmoid(x)


def _layer_norm(z, g, b):
    mu = jnp.mean(z, axis=-1, keepdims=True)
    zc = z - mu
    var = jnp.mean(zc * zc, axis=-1, keepdims=True)
    return zc * lax.rsqrt(var + LN_EPS) * g + b


def _rms_norm(x, g):
    return x * lax.rsqrt(jnp.mean(x * x, axis=-1, keepdims=True) + LN_EPS) * g


def _arb(n):
    return pltpu.CompilerParams(dimension_semantics=("arbitrary",) * n)


ADA_TN = 512


def _ada_body(c_ref, w_ref, b_ref, o_ref):
    a = _silu(c_ref[...]).astype(BF16)
    o_ref[...] = jnp.dot(a, w_ref[...].astype(BF16), preferred_element_type=F32) + b_ref[...]


def _ada(c_all, w_ada, b_ada):
    rows = c_all.shape[0]
    ncol = w_ada.shape[2]
    return pl.pallas_call(
        _ada_body,
        grid=(DEPTH, ncol // ADA_TN),
        in_specs=[
            pl.BlockSpec((rows, D_MODEL), lambda l, j: (0, 0)),
            pl.BlockSpec((None, D_MODEL, ADA_TN), lambda l, j: (l, 0, j)),
            pl.BlockSpec((None, 1, ADA_TN), lambda l, j: (l, 0, j)),
        ],
        out_specs=pl.BlockSpec((None, rows, ADA_TN), lambda l, j: (l, 0, j)),
        out_shape=jax.ShapeDtypeStruct((DEPTH, rows, ncol), F32),
        compiler_params=_arb(2),
        name="ada",
    )(c_all, w_ada, b_ada.reshape(DEPTH, 1, ncol))


def _mod_spec(k, tiles_per_batch, r):
    return pl.BlockSpec((None, r, D_MODEL), lambda i: ((i // tiles_per_batch) * 6 + k, 0, 0))


def _proj_body(x_ref, sh_ref, sc_ref, w_ref, o_ref, wb):
    @pl.when(pl.program_id(0) == 0)
    def _():
        wb[...] = w_ref[...].astype(BF16)

    u = x_ref[...] * (1.0 + sc_ref[...]) + sh_ref[...]
    o_ref[...] = jnp.dot(u.astype(BF16), wb[...], preferred_element_type=F32)


def _proj(x, mod, w_in_l, tr, tiles_per_batch):
    t = x.shape[0]
    r = mod.shape[1]
    return pl.pallas_call(
        _proj_body,
        grid=(t // tr,),
        in_specs=[
            pl.BlockSpec((tr, D_MODEL), lambda i: (i, 0)),
            _mod_spec(0, tiles_per_batch, r),
            _mod_spec(1, tiles_per_batch, r),
            pl.BlockSpec((D_MODEL, IN_COLS), lambda i: (0, 0)),
        ],
        out_specs=pl.BlockSpec((tr, IN_COLS), lambda i: (i, 0)),
        out_shape=jax.ShapeDtypeStruct((t, IN_COLS), F32),
        scratch_shapes=[pltpu.VMEM((D_MODEL, IN_COLS), BF16)],
        compiler_params=_arb(1),
        name="proj",
    )(x, mod, mod, w_in_l)


def _band_unit(q, kw, vw, bias):
    lane = lax.broadcasted_iota(I32, (BAND, LANES), 1)
    qs = q * ATTN_SCALE
    q2 = jnp.concatenate([jnp.where(lane < HEAD_DIM, qs, 0.0), jnp.where(lane >= HEAD_DIM, qs, 0.0)], axis=0)
    s = lax.dot_general(q2.astype(BF16), kw.astype(BF16), (((1,), (1,)), ((), ())), preferred_element_type=F32)
    s = s + bias
    m = jnp.max(s, axis=1, keepdims=True)
    pe = jnp.exp(s - m).astype(BF16)
    vx = jnp.concatenate([vw.astype(BF16), jnp.ones((2 * BAND, LANES), BF16)], axis=1)
    pvx = jnp.dot(pe, vx, preferred_element_type=F32)
    return pvx[:, :LANES], m, pvx[:, LANES:]


def _pair_merge(a):
    lane = lax.broadcasted_iota(I32, (BAND, LANES), 1)
    return jnp.where(lane < HEAD_DIM, jnp.broadcast_to(a[:BAND], (BAND, LANES)),
                     jnp.broadcast_to(a[BAND:], (BAND, LANES)))


def _make_bias(slope0, slope1, d, first):
    row = lax.broadcasted_iota(I32, (2 * BAND, 2 * BAND), 0)
    col = lax.broadcasted_iota(I32, (2 * BAND, 2 * BAND), 1)
    i = jnp.where(row >= BAND, row - BAND, row)
    dist = i + BAND - col
    valid = (dist >= 0) & (dist <= BAND)
    if first:
        valid = valid & (col >= BAND)
    slope = jnp.where(row >= BAND, slope1, slope0)
    return jnp.where(valid, -(slope * (dist * d).astype(F32)), NEG)


def _attn_a_body(slopes_ref, q_ref, k_ref, v_ref, o_ref, qa, ka, va, ktail, vtail, acc, ms, ls, bias):
    p = pl.program_id(1)
    st = pl.program_id(2)
    cur = st % 2
    prv = 1 - cur

    @pl.when(st == 0)
    def _():
        ka[1] = jnp.zeros((4, Q4, LANES), F32)
        va[1] = jnp.zeros((4, Q4, LANES), F32)
        ktail[...] = jnp.zeros((BAND, LANES), F32)
        vtail[...] = jnp.zeros((BAND, LANES), F32)
        s0 = slopes_ref[2 * p]
        s1 = slopes_ref[2 * p + 1]
        for ci, (_, d) in enumerate(DILATIONS):
            bias[ci, 0] = _make_bias(s0, s1, d, False)
            bias[ci, 1] = _make_bias(s0, s1, d, True)

    for c in range(4):
        qa[c] = q_ref[pl.ds(c, Q4, stride=4), :]
        ka[cur, c] = k_ref[pl.ds(c, Q4, stride=4), :]
        va[cur, c] = v_ref[pl.ds(c, Q4, stride=4), :]

    first0 = (st == 0).astype(I32)
    n_groups = SUP // BAND // UNROLL

    def store(ci, idx, pv, m, l):
        acc[ci, idx, :] = _pair_merge(pv)
        ms[ci, idx, :] = _pair_merge(m)
        ls[ci, idx, :] = _pair_merge(l)

    pv, m, l = _band_unit(q_ref[0:BAND, :], jnp.concatenate([ktail[...], k_ref[0:BAND, :]], axis=0),
                          jnp.concatenate([vtail[...], v_ref[0:BAND, :]], axis=0), bias[0, first0])
    store(0, pl.ds(0, BAND), pv, m, l)

    def unit1(n):
        qs = pl.multiple_of(n * BAND, BAND)
        ks = pl.multiple_of(n * BAND - BAND, BAND)
        pv, m, l = _band_unit(q_ref[pl.ds(qs, BAND), :], k_ref[pl.ds(ks, 2 * BAND), :],
                              v_ref[pl.ds(ks, 2 * BAND), :], bias[0, 0])
        store(0, pl.ds(qs, BAND), pv, m, l)

    for n in range(1, UNROLL):
        unit1(n)

    def group1(g, carry):
        for j in range(UNROLL):
            unit1(g * UNROLL + j)
        return carry
    lax.fori_loop(1, n_groups, group1, 0)
    ktail[...] = k_ref[SUP - BAND:SUP, :]
    vtail[...] = v_ref[SUP - BAND:SUP, :]

    def unit4(u, r, first):
        qs = pl.multiple_of(u * BAND, BAND)
        if first:
            kw = jnp.concatenate([ka[prv, r, Q4 - BAND:Q4, :], ka[cur, r, 0:BAND, :]], axis=0)
            vw = jnp.concatenate([va[prv, r, Q4 - BAND:Q4, :], va[cur, r, 0:BAND, :]], axis=0)
            b = bias[1, first0]
        else:
            kw = ka[cur, r, pl.ds(qs - BAND, 2 * BAND), :]
            vw = va[cur, r, pl.ds(qs - BAND, 2 * BAND), :]
            b = bias[1, 0]
        pv, m, l = _band_unit(qa[r, pl.ds(qs, BAND), :], kw, vw, b)
        store(1, pl.ds(r * Q4 + qs, BAND), pv, m, l)

    for n in range(UNROLL):
        unit4(n // 4, n % 4, n < 4)

    def group4(g, carry):
        for j in range(UNROLL):
            unit4(g * (UNROLL // 4) + j // 4, j % 4, False)
        return carry
    lax.fori_loop(1, n_groups, group4, 0)

    def group16(g, carry):
        for j in range(UNROLL):
            c = g * (UNROLL // 4) + j // 4
            a = j % 4
            idx = pl.ds(a, BAND, stride=4)
            kw = jnp.concatenate([ka[prv, c, idx, :], ka[cur, c, idx, :]], axis=0)
            vw = jnp.concatenate([va[prv, c, idx, :], va[cur, c, idx, :]], axis=0)
            pv, m, l = _band_unit(qa[c, idx, :], kw, vw, bias[2, first0])
            store(2, pl.ds(c * Q4 + a, BAND, stride=4), pv, m, l)
        return carry
    lax.fori_loop(0, n_groups, group16, 0)

    def fin(i, carry):
        c = i // 4
        rows = pl.ds(pl.multiple_of(i * BAND, BAND), BAND)
        tok = pl.ds(c + 4 * BAND * (i - 4 * c), BAND, stride=4)
        m0, m1, m2 = ms[0, tok, :], ms[1, rows, :], ms[2, rows, :]
        mm = jnp.maximum(jnp.maximum(m0, m1), m2)
        w0, w1, w2 = jnp.exp(m0 - mm), jnp.exp(m1 - mm), jnp.exp(m2 - mm)
        den = w0 * ls[0, tok, :] + w1 * ls[1, rows, :] + w2 * ls[2, rows, :]
        num = w0 * acc[0, tok, :] + w1 * acc[1, rows, :] + w2 * acc[2, rows, :]
        o_ref[tok, :] = num / den
        return carry
    lax.fori_loop(0, SUP // BAND, fin, 0)


def _attn_a(h, slopes, nbatch, seq):
    assert UNROLL % 4 == 0 and (SUP // BAND) % UNROLL == 0 and seq % SUP == 0
    nst = seq // SUP
    npair = N_HEADS // 2

    def col(c0):
        return lambda b, p, st: (b * nst + st, c0 + p)

    return pl.pallas_call(
        _attn_a_body,
        grid=(nbatch, npair, nst),
        in_specs=[
            pl.BlockSpec(memory_space=pltpu.SMEM),
            pl.BlockSpec((SUP, LANES), col(0)),
            pl.BlockSpec((SUP, LANES), col(npair)),
            pl.BlockSpec((SUP, LANES), col(2 * npair)),
        ],
        out_specs=pl.BlockSpec((SUP, LANES), col(0)),
        out_shape=jax.ShapeDtypeStruct((nbatch * seq, A_WIDTH), F32),
        scratch_shapes=[
            pltpu.VMEM((4, Q4, LANES), F32),
            pltpu.VMEM((2, 4, Q4, LANES), F32),
            pltpu.VMEM((2, 4, Q4, LANES), F32),
            pltpu.VMEM((BAND, LANES), F32),
            pltpu.VMEM((BAND, LANES), F32),
            pltpu.VMEM((3, SUP, LANES), F32),
            pltpu.VMEM((3, SUP, LANES), F32),
            pltpu.VMEM((3, SUP, LANES), F32),
            pltpu.VMEM((3, 2, 2 * BAND, 2 * BAND), F32),
        ],
        compiler_params=_arb(3),
        name="attn_a",
    )(slopes, h, h, h)


def _attn_b_body(slopes_ref, sinks_ref, q_ref, k_ref, v_ref, o_ref, k2, v2, bias):
    p = pl.program_id(1)
    st = pl.program_id(2)

    @pl.when(st == 0)
    def _():
        k2[0:BAND, :] = jnp.zeros((BAND, LANES), F32)
        v2[0:BAND, :] = jnp.zeros((BAND, LANES), F32)
        s0 = slopes_ref[2 * p]
        s1 = slopes_ref[2 * p + 1]
        bias[0] = _make_bias(s0, s1, 1, False)
        bias[1] = _make_bias(s0, s1, 1, True)

    @pl.when(st > 0)
    def _():
        k2[0:BAND, :] = k2[SUP:SUP + BAND, :]
        v2[0:BAND, :] = v2[SUP:SUP + BAND, :]

    lane = lax.broadcasted_iota(I32, (SUP, LANES), 1)
    keep = (lane < HEAD_DIM) == (p < G_B // 2)
    k2[BAND:BAND + SUP, :] = jnp.where(keep, k_ref[...], pltpu.roll(k_ref[...], HEAD_DIM, 1))
    v2[BAND:BAND + SUP, :] = jnp.where(keep, v_ref[...], pltpu.roll(v_ref[...], HEAD_DIM, 1))

    row = lax.broadcasted_iota(I32, (2 * BAND, 1), 0)
    sink = jnp.where(row >= BAND, sinks_ref[2 * p + 1], sinks_ref[2 * p])

    def group(g, carry):
        for j in range(UNROLL):
            n = g * UNROLL + j
            qstart = pl.multiple_of(n * BAND, BAND)
            first = jnp.logical_and(st == 0, n == 0).astype(I32)
            pv, m, l = _band_unit(q_ref[pl.ds(qstart, BAND), :], k2[pl.ds(qstart, 2 * BAND), :],
                                  v2[pl.ds(qstart, 2 * BAND), :], bias[first])
            m2 = jnp.maximum(m, sink)
            scale = jnp.exp(m - m2)
            den = l * scale + jnp.exp(sink - m2)
            o_ref[pl.ds(qstart, BAND), :] = _pair_merge(pv * scale / den)
        return carry
    lax.fori_loop(0, SUP // BAND // UNROLL, group, 0)


def _attn_b(h, slopes, sinks_l, nbatch, seq):
    nst = seq // SUP
    npair = N_HEADS // 2
    qcol0 = 3 * npair
    kcol = 4 * npair
    vcol = kcol + 1
    return pl.pallas_call(
        _attn_b_body,
        grid=(nbatch, npair, nst),
        in_specs=[
            pl.BlockSpec(memory_space=pltpu.SMEM),
            pl.BlockSpec(memory_space=pltpu.SMEM),
            pl.BlockSpec((SUP, LANES), lambda b, p, st: (b * nst + st, qcol0 + p)),
            pl.BlockSpec((SUP, LANES), lambda b, p, st: (b * nst + st, kcol)),
            pl.BlockSpec((SUP, LANES), lambda b, p, st: (b * nst + st, vcol)),
        ],
        out_specs=pl.BlockSpec((SUP, LANES), lambda b, p, st: (b * nst + st, p)),
        out_shape=jax.ShapeDtypeStruct((nbatch * seq, A_WIDTH), F32),
        scratch_shapes=[
            pltpu.VMEM((BAND + SUP, LANES), F32),
            pltpu.VMEM((BAND + SUP, LANES), F32),
            pltpu.VMEM((2, 2 * BAND, 2 * BAND), F32),
        ],
        compiler_params=_arb(3),
        name="attn_b",
    )(slopes, sinks_l, h, h, h)


def _route(logits_t, rb):
    aff = jax.nn.sigmoid(logits_t)
    sel = aff + rb
    a = [aff[SUBLANES * j:SUBLANES * (j + 1)] for j in range(EPG)]
    s = [sel[SUBLANES * j:SUBLANES * (j + 1)] for j in range(EPG)]
    hi1, lo1 = jnp.maximum(s[0], s[1]), jnp.minimum(s[0], s[1])
    hi2, lo2 = jnp.maximum(s[2], s[3]), jnp.minimum(s[2], s[3])
    top1 = jnp.maximum(hi1, hi2)
    top2 = jnp.maximum(jnp.minimum(hi1, hi2), jnp.maximum(lo1, lo2))
    gscore = top1 + top2
    giota = lax.broadcasted_iota(I32, gscore.shape, 0)
    gmax = jnp.max(gscore, axis=0, keepdims=True)
    gbest = jnp.min(jnp.where(gscore == gmax, giota, N_GROUPS), axis=0, keepdims=True)
    gmask = giota == gbest
    v = [jnp.sum(jnp.where(gmask, sj, 0.0), axis=0, keepdims=True) for sj in s]
    av = [jnp.sum(jnp.where(gmask, aj, 0.0), axis=0, keepdims=True) for aj in a]
    m1 = jnp.maximum(jnp.maximum(v[0], v[1]), jnp.maximum(v[2], v[3]))
    i1 = jnp.where(v[0] == m1, 0, jnp.where(v[1] == m1, 1, jnp.where(v[2] == m1, 2, 3)))
    w = [jnp.where(i1 == j, -jnp.inf, v[j]) for j in range(EPG)]
    m2 = jnp.maximum(jnp.maximum(w[0], w[1]), jnp.maximum(w[2], w[3]))
    i2 = jnp.where(w[0] == m2, 0, jnp.where(w[1] == m2, 1, jnp.where(w[2] == m2, 2, 3)))
    a1 = sum(jnp.where(i1 == j, av[j], 0.0) for j in range(EPG))
    a2 = sum(jnp.where(i2 == j, av[j], 0.0) for j in range(EPG))
    den = a1 + a2
    comb = [jnp.where(i1 == j, a1 / den, 0.0) + jnp.where(i2 == j, a2 / den, 0.0) for j in range(EPG)]
    return gbest, comb


RUN_ALIGN = SUBLANES
LP_LANE = EPG


def _round_up_i32(x, m):
    shift = int(np.log2(m))
    return ((x + (m - 1)) >> shift) << shift


def _group_starts(padded):
    giota = lax.broadcasted_iota(I32, padded.shape, 0)
    off = jnp.zeros(padded.shape, I32)
    for g in range(N_GROUPS - 1):
        off = off + jnp.where(giota > g, padded[g:g + 1, :], 0)
    return off


def _merge_body(oa_ref, ob_ref, x_ref, gt_ref, sh_ref, sc_ref, ga_ref, gb_ref, w_ref, lg_ref, lb_ref,
                rw_ref, rb_ref, x1_ref, u2_ref, tail_ref, lp_ref, cnt_ref, wb, tri, *, sbk):
    @pl.when(pl.program_id(0) == 0)
    def _():
        wb[...] = w_ref[...].astype(BF16)
        r_i = lax.broadcasted_iota(I32, (sbk, sbk), 0)
        c_i = lax.broadcasted_iota(I32, (sbk, sbk), 1)
        tri[...] = (r_i <= c_i).astype(BF16)

    def rows_of(ref, rows):
        return ref[...] if ref.shape[0] == 1 else ref[rows, :]

    for k in range(x_ref.shape[0] // sbk):
        rows = slice(k * sbk, (k + 1) * sbk)
        ya = _rms_norm(oa_ref[rows, :], ga_ref[...])
        yb = _rms_norm(ob_ref[rows, :], gb_ref[...])
        y = jnp.concatenate([ya, yb], axis=-1).astype(BF16)
        mix = jnp.dot(y, wb[...], preferred_element_type=F32)
        x1 = _layer_norm(ALPHA * x_ref[rows, :] + rows_of(gt_ref, rows) * mix, lg_ref[...], lb_ref[...])
        x1_ref[rows, :] = x1
        u2 = x1 * (1.0 + rows_of(sc_ref, rows)) + rows_of(sh_ref, rows)
        u2_ref[rows, :] = u2

        logits_t = lax.dot_general(rw_ref[...], u2, (((1,), (1,)), ((), ())), precision=lax.Precision.HIGHEST,
                                   preferred_element_type=F32)
        gbest, comb = _route(logits_t, rb_ref[...])
        onehot = lax.broadcasted_iota(I32, (N_GROUPS, sbk), 0) == gbest
        cnt = jnp.sum(onehot.astype(F32), axis=1, keepdims=True)
        starts = _group_starts(_round_up_i32(cnt.astype(I32), RUN_ALIGN)).astype(F32)
        inc = jnp.dot(onehot.astype(BF16), tri[...], preferred_element_type=F32)
        lp = jnp.sum(jnp.where(onehot, starts + inc - 1.0, 0.0), axis=0, keepdims=True)
        lp_ref[k] = lp
        cnt_ref[k] = jnp.broadcast_to(cnt, (N_GROUPS, LANES))
        row = lax.broadcasted_iota(I32, (TAIL, sbk), 0)
        slab = jnp.where(row == LP_LANE, lp, 0.0)
        for j in range(EPG):
            slab = jnp.where(row == j, comb[j], slab)
        tail_ref[rows, :] = slab.T


def _merge(oa, ob, x, mod, ga, gb, w_out_l, lg, lb, rw_t, rb, tr, tiles_per_batch, sbk):
    t = x.shape[0]
    r = mod.shape[1]
    nt = t // tr
    nsb = tr // sbk
    row = lambda i: (i, 0)
    fixed = lambda i: (0, 0)
    return pl.pallas_call(
        functools.partial(_merge_body, sbk=sbk),
        grid=(nt,),
        in_specs=[
            pl.BlockSpec((tr, A_WIDTH), row),
            pl.BlockSpec((tr, A_WIDTH), row),
            pl.BlockSpec((tr, D_MODEL), row),
            _mod_spec(2, tiles_per_batch, r),
            _mod_spec(3, tiles_per_batch, r),
            _mod_spec(4, tiles_per_batch, r),
            pl.BlockSpec((1, A_WIDTH), fixed),
            pl.BlockSpec((1, A_WIDTH), fixed),
            pl.BlockSpec((D_MODEL, D_MODEL), fixed),
            pl.BlockSpec((1, D_MODEL), fixed),
            pl.BlockSpec((1, D_MODEL), fixed),
            pl.BlockSpec((N_EXPERTS, D_MODEL), fixed),
            pl.BlockSpec((N_EXPERTS, 1), fixed),
        ],
        out_specs=[
            pl.BlockSpec((tr, D_MODEL), row),
            pl.BlockSpec((tr, D_MODEL), row),
            pl.BlockSpec((tr, TAIL), row),
            pl.BlockSpec((nsb, 1, sbk), lambda i: (i, 0, 0)),
            pl.BlockSpec((nsb, N_GROUPS, LANES), lambda i: (i, 0, 0)),
        ],
        out_shape=[
            jax.ShapeDtypeStruct((t, D_MODEL), F32),
            jax.ShapeDtypeStruct((t, D_MODEL), F32),
            jax.ShapeDtypeStruct((t, TAIL), F32),
            jax.ShapeDtypeStruct((t // sbk, 1, sbk), F32),
            jax.ShapeDtypeStruct((t // sbk, N_GROUPS, LANES), F32),
        ],
        scratch_shapes=[pltpu.VMEM((D_MODEL, D_MODEL), BF16), pltpu.VMEM((sbk, sbk), BF16)],
        compiler_params=_arb(1),
        name="merge",
    )(oa, ob, x, mod, mod, mod, ga, gb, w_out_l, lg, lb, rw_t, rb)


PLAN_SIZE, PLAN_LOCAL, PLAN_GLOBAL = 0, N_GROUPS, 2 * N_GROUPS


def _plan_body(cnt_ref, plan_ref, meta_ref, run_off, *, tm):
    nt = cnt_ref.shape[0]

    def sizes(i):
        return _round_up_i32(cnt_ref[i].astype(I32), RUN_ALIGN)

    def first(i, total):
        run_off[i] = total
        return total + sizes(i)
    totals = lax.fori_loop(0, nt, first, jnp.zeros((N_GROUPS, LANES), I32))
    padded = _round_up_i32(totals, tm)
    goff = _group_starts(padded)
    ends = goff + padded
    giota = lax.broadcasted_iota(I32, (N_GROUPS, LANES), 0)
    lane = lax.broadcasted_iota(I32, (N_GROUPS, LANES), 1)
    tile_group = jnp.sum((ends <= lane * tm).astype(F32), axis=0, keepdims=True).astype(I32)
    tile_group = jnp.minimum(tile_group, N_GROUPS - 1)
    n_used = ends[N_GROUPS - 1:N_GROUPS, :] >> int(np.log2(tm))
    meta_ref[...] = jnp.where(giota == 0, tile_group, jnp.where(giota == 1, n_used, 0))

    def on_lanes(v):
        picked = jnp.where(giota == (lane & (N_GROUPS - 1)), v, 0).astype(F32)
        return jnp.sum(picked, axis=0, keepdims=True).astype(I32)

    lane1 = lax.broadcasted_iota(I32, (1, LANES), 1)

    def second(i, carry):
        sz = sizes(i)
        row = jnp.where(lane1 < PLAN_LOCAL, on_lanes(sz),
                        jnp.where(lane1 < PLAN_GLOBAL, on_lanes(_group_starts(sz)), on_lanes(goff + run_off[i])))
        plan_ref[i] = row
        return carry
    lax.fori_loop(0, nt, second, 0)


def _plan(cnt, tm):
    nt = cnt.shape[0]
    return pl.pallas_call(
        functools.partial(_plan_body, tm=tm),
        out_shape=[jax.ShapeDtypeStruct((nt, 1, LANES), I32), jax.ShapeDtypeStruct((N_GROUPS, LANES), I32)],
        scratch_shapes=[pltpu.VMEM((nt, N_GROUPS, LANES), I32)],
        name="plan",
    )(cnt)


def _run_copies(plan_ref, sbk, make):
    out = []
    sizes = [s for s in (256, 128, 64, 32, 16, 8) if s <= sbk]
    for g in range(N_GROUPS):
        n = plan_ref[0, PLAN_SIZE + g]
        loc = plan_ref[0, PLAN_LOCAL + g]
        glo = plan_ref[0, PLAN_GLOBAL + g]
        done = jnp.int32(0)
        for s in sizes:
            cond = (n & s) != 0
            out.append((cond, make(pl.multiple_of(loc + done, RUN_ALIGN), pl.multiple_of(glo + done, RUN_ALIGN), s)))
            done = done + jnp.where(cond, s, 0)
    return out


def _start(copies):
    for cond, cp in copies:
        pl.when(cond)(cp.start)


def _wait(copies):
    for cond, cp in copies:
        pl.when(cond)(cp.wait)


def _split_bf16(x):
    hi = x.astype(BF16)
    return hi, (x - hi.astype(F32)).astype(BF16)


def _scatter_body(plan_ref, plan_prev_ref, lp_ref, u_ref, tail_ref, dst_in, dst_ref, buf, sem):
    del dst_in
    i = pl.program_id(0)
    slot = i % 2
    sbk = u_ref.shape[0]
    lr = buf.shape[1]

    def copies(plan, s):
        return _run_copies(plan, sbk, lambda loc, glo, n: pltpu.make_async_copy(
            buf.at[s, pl.ds(loc, n), :], dst_ref.at[pl.ds(glo, n), :], sem.at[s]))

    perm = (lax.broadcasted_iota(I32, (lr, sbk), 0) == lp_ref[...].astype(I32)).astype(BF16)
    t_hi, t_lo = _split_bf16(tail_ref[...])
    x = jnp.concatenate([u_ref[...].astype(BF16), t_hi, t_lo], axis=1)
    y = jnp.dot(perm, x, preferred_element_type=F32)
    buf[slot, :, 0:D_MODEL] = y[:, 0:D_MODEL]
    buf[slot, :, D_MODEL:ROW_W] = y[:, D_MODEL:ROW_W] + y[:, ROW_W:ROW_W + TAIL]
    _start(copies(plan_ref, slot))

    @pl.when(i > 0)
    def _():
        _wait(copies(plan_prev_ref, 1 - slot))

    @pl.when(i == pl.num_programs(0) - 1)
    def _():
        _wait(copies(plan_ref, slot))


def _scatter_rows(plan, lp, u2, tail, n_sorted, sbk):
    t = u2.shape[0]
    dst = jnp.zeros((n_sorted, ROW_W), F32)
    return pl.pallas_call(
        _scatter_body,
        grid=(t // sbk,),
        in_specs=[
            pl.BlockSpec((None, 1, LANES), lambda i: (i, 0, 0), memory_space=pltpu.SMEM),
            pl.BlockSpec((None, 1, LANES), lambda i: (jnp.maximum(i - 1, 0), 0, 0), memory_space=pltpu.SMEM),
            pl.BlockSpec((None, 1, sbk), lambda i: (i, 0, 0)),
            pl.BlockSpec((sbk, D_MODEL), lambda i: (i, 0)),
            pl.BlockSpec((sbk, TAIL), lambda i: (i, 0)),
            pl.BlockSpec(memory_space=pl.ANY),
        ],
        out_specs=pl.BlockSpec(memory_space=pl.ANY),
        out_shape=jax.ShapeDtypeStruct((n_sorted, ROW_W), F32),
        scratch_shapes=[pltpu.VMEM((2, sbk + LANES, ROW_W), F32), pltpu.SemaphoreType.DMA((2,))],
        input_output_aliases={5: 0},
        compiler_params=_arb(1),
        name="scatter_rows",
    )(plan, plan, lp, u2, tail, dst)


def _combine_body(plan_ref, plan_next_ref, ys_ref, tail_ref, x1_ref, gt_ref, lg_ref, lb_ref, o_ref, buf, sem):
    i = pl.program_id(0)
    slot = i % 2
    sbk = x1_ref.shape[0]
    lr = buf.shape[1]

    def copies(plan, s):
        return _run_copies(plan, sbk, lambda loc, glo, n: pltpu.make_async_copy(
            ys_ref.at[pl.ds(glo, n), :], buf.at[s, pl.ds(loc, n), :], sem.at[s]))

    @pl.when(i == 0)
    def _():
        _start(copies(plan_ref, slot))

    @pl.when(i < pl.num_programs(0) - 1)
    def _():
        _start(copies(plan_next_ref, 1 - slot))

    _wait(copies(plan_ref, slot))
    n_rows = plan_ref[0, PLAN_LOCAL + N_GROUPS - 1] + plan_ref[0, PLAN_SIZE + N_GROUPS - 1]
    ys = jnp.where(lax.broadcasted_iota(I32, (lr, 1), 0) < n_rows, buf[slot], 0.0)
    lp = tail_ref[:, LP_LANE:LP_LANE + 1].astype(I32)
    perm_t = (lax.broadcasted_iota(I32, (sbk, lr), 1) == lp).astype(BF16)
    y_hi, y_lo = _split_bf16(ys)
    y = jnp.dot(perm_t, y_hi, preferred_element_type=F32) + jnp.dot(perm_t, y_lo, preferred_element_type=F32)
    o_ref[...] = _layer_norm(ALPHA * x1_ref[...] + gt_ref[...] * y, lg_ref[...], lb_ref[...])


def _combine(plan, ys, tail, x1, mod, lg, lb, sbk, tiles_per_batch):
    t = x1.shape[0]
    r = mod.shape[1]
    nt = t // sbk
    return pl.pallas_call(
        _combine_body,
        grid=(nt,),
        in_specs=[
            pl.BlockSpec((None, 1, LANES), lambda i: (i, 0, 0), memory_space=pltpu.SMEM),
            pl.BlockSpec((None, 1, LANES), lambda i: (jnp.minimum(i + 1, nt - 1), 0, 0), memory_space=pltpu.SMEM),
            pl.BlockSpec(memory_space=pl.ANY),
            pl.BlockSpec((sbk, TAIL), lambda i: (i, 0)),
            pl.BlockSpec((sbk, D_MODEL), lambda i: (i, 0)),
            _mod_spec(5, tiles_per_batch, r),
            pl.BlockSpec((1, D_MODEL), lambda i: (0, 0)),
            pl.BlockSpec((1, D_MODEL), lambda i: (0, 0)),
        ],
        out_specs=pl.BlockSpec((sbk, D_MODEL), lambda i: (i, 0)),
        out_shape=jax.ShapeDtypeStruct((t, D_MODEL), F32),
        scratch_shapes=[pltpu.VMEM((2, sbk + LANES, D_MODEL), F32), pltpu.SemaphoreType.DMA((2,))],
        compiler_params=_arb(1),
        name="combine",
    )(plan, plan, ys, tail, x1, mod, lg, lb)


def _moe_body(tg_ref, nu_ref, xs_ref, wg_ref, wu_ref, wd_ref, ys_ref, wgb, wub, wdb):
    i = pl.program_id(0)
    prev = tg_ref[jnp.maximum(i - 1, 0)]

    @pl.when(jnp.logical_or(i == 0, tg_ref[i] != prev))
    def _():
        wgb[...] = wg_ref[...].astype(BF16)
        wub[...] = wu_ref[...].astype(BF16)
        wdb[...] = wd_ref[...].astype(BF16)

    @pl.when(i < nu_ref[0])
    def _():
        x = xs_ref[:, 0:D_MODEL].astype(BF16)
        comb = xs_ref[:, D_MODEL:ROW_W]
        y = jnp.zeros((x.shape[0], D_MODEL), F32)
        for e in range(EPG):
            hg = jnp.dot(x, wgb[e], preferred_element_type=F32)
            hu = jnp.dot(x, wub[e], preferred_element_type=F32)
            act = _silu(hg) * hu * comb[:, e:e + 1]
            y = y + jnp.dot(act.astype(BF16), wdb[e], preferred_element_type=F32)
        ys_ref[...] = y

    @pl.when(i >= nu_ref[0])
    def _():
        ys_ref[...] = jnp.zeros(ys_ref.shape, F32)


def _moe(tile_group, n_used, xs, w_gate, w_up, w_down, layer, tm):
    n_sorted = xs.shape[0]
    nt = n_sorted // tm
    grid_spec = pltpu.PrefetchScalarGridSpec(
        num_scalar_prefetch=2,
        grid=(nt,),
        in_specs=[
            pl.BlockSpec((tm, ROW_W), lambda i, tg, nu: (i, 0)),
            pl.BlockSpec((None, EPG, D_MODEL, D_EXPERT), lambda i, tg, nu: (layer, tg[i], 0, 0)),
            pl.BlockSpec((None, EPG, D_MODEL, D_EXPERT), lambda i, tg, nu: (layer, tg[i], 0, 0)),
            pl.BlockSpec((None, EPG, D_EXPERT, D_MODEL), lambda i, tg, nu: (layer, tg[i], 0, 0)),
        ],
        out_specs=pl.BlockSpec((tm, D_MODEL), lambda i, tg, nu: (i, 0)),
        scratch_shapes=[
            pltpu.VMEM((EPG, D_MODEL, D_EXPERT), BF16),
            pltpu.VMEM((EPG, D_MODEL, D_EXPERT), BF16),
            pltpu.VMEM((EPG, D_EXPERT, D_MODEL), BF16),
        ],
    )
    return pl.pallas_call(
        _moe_body,
        grid_spec=grid_spec,
        out_shape=jax.ShapeDtypeStruct((n_sorted, D_MODEL), F32),
        compiler_params=_arb(1),
        name="moe",
    )(tile_group, n_used, xs, w_gate, w_up, w_down)


def _moe_layer(u2, tail, lp, cnt, x1, mod, w_gate, w_up, w_down, lg, lb, layer, sbk, tiles_per_batch, tm):
    t = x1.shape[0]
    nt = t // sbk
    n_tiles = -(-(t + nt * N_GROUPS * (RUN_ALIGN - 1)) // tm) + N_GROUPS
    assert n_tiles <= LANES
    plan, meta = _plan(cnt, tm)
    xs = _scatter_rows(plan, lp, u2, tail, n_tiles * tm, sbk)
    ys = _moe(meta[0, :n_tiles], meta[1, :1], xs, w_gate, w_up, w_down, layer, tm)
    return _combine(plan, ys, tail, x1, mod, lg, lb, sbk, tiles_per_batch)


def _shift_in(x, new_col):
    n = x.shape[-1]
    lane = lax.broadcasted_iota(I32, x.shape, x.ndim - 1)
    return jnp.where(lane == n - 1, new_col, pltpu.roll(x, n - 1, x.ndim - 1))


def _decode(q_t, k_of, v_of, kn_t, vn_t, head_kv, slope, mult, mult_new, sink):
    n = mult.shape[1]
    rows, rows_new = [], []
    for h in range(N_HEADS):
        g = head_kv[h]
        qh = q_t[:, h:h + 1]
        rows.append(jnp.sum(qh * k_of(g), axis=0, keepdims=True))
        rows_new.append(jnp.sum(qh * kn_t[:, g:g + 1], axis=0, keepdims=True))
    pos = lax.broadcasted_iota(I32, (1, n), 1)
    s = jnp.concatenate(rows, axis=0) * ATTN_SCALE - slope * (n - pos).astype(F32)
    s = jnp.where(mult > 0.0, s, NEG)
    s_new = jnp.concatenate(rows_new, axis=0) * ATTN_SCALE
    m = jnp.maximum(jnp.max(s, axis=1, keepdims=True), s_new)
    if sink is not None:
        m = jnp.maximum(m, sink)
    p = mult * jnp.exp(s - m)
    p_new = mult_new * jnp.exp(s_new - m)
    den = jnp.sum(p, axis=1, keepdims=True) + p_new
    if sink is not None:
        den = den + jnp.exp(sink - m)
    w = p / den
    w_new = p_new / den
    cols = []
    for h in range(N_HEADS):
        g = head_kv[h]
        cols.append(jnp.sum(w[h:h + 1, :] * v_of(g), axis=1, keepdims=True) + w_new[h:h + 1, :] * vn_t[:, g:g + 1])
    return jnp.concatenate(cols, axis=1)


def _dilated_multiplicity(n):
    pos = lax.broadcasted_iota(I32, (1, n), 1)
    dist = n - pos
    mult = jnp.zeros((1, n), F32)
    for (w, d) in DILATIONS:
        mult = mult + jnp.where((dist <= w) & ((dist & (d - 1)) == 0), 1.0, 0.0)
    return mult


N_SAMPLE_IN = 7
HT_COLS = IN_COLS // HEAD_DIM


def _sample_body(*refs):
    slope_ref, sink_ref, ht_ref, ka_ref, va_ref, kb_ref, vb_ref = refs[:N_SAMPLE_IN]
    o_ref, oka_ref, ova_ref, okb_ref, ovb_ref = refs[-5:]

    slope = slope_ref[...]
    ht = ht_ref[...]
    nh = N_HEADS
    qa, kna, vna, qb = ht[:, 0:nh], ht[:, nh:2 * nh], ht[:, 2 * nh:3 * nh], ht[:, 3 * nh:4 * nh]
    knb, vnb = ht[:, 4 * nh:4 * nh + KV_B], ht[:, 4 * nh + KV_B:4 * nh + 2 * KV_B]
    o_ref[:, 0:nh] = _decode(qa, lambda g: ka_ref[g], lambda g: va_ref[g], kna, vna,
                             list(range(N_HEADS)), slope, _dilated_multiplicity(WIN_A), float(len(DILATIONS)), None)
    o_ref[:, nh:2 * nh] = _decode(qb, lambda g: kb_ref[g], lambda g: vb_ref[g], knb, vnb,
                                  [h // G_B for h in range(N_HEADS)], slope, jnp.ones((1, WIN_B), F32), 1.0,
                                  sink_ref[...])

    def shift(h, carry):
        hot = lax.broadcasted_iota(I32, (1, N_HEADS), 1) == h
        oka_ref[h] = _shift_in(ka_ref[h], jnp.sum(jnp.where(hot, kna, 0.0), axis=1, keepdims=True))
        ova_ref[h] = _shift_in(va_ref[h], jnp.sum(jnp.where(hot, vna, 0.0), axis=1, keepdims=True))
        return carry
    lax.fori_loop(0, N_HEADS, shift, 0)
    for g in range(KV_B):
        okb_ref[g] = _shift_in(kb_ref[g], knb[:, g:g + 1])
        ovb_ref[g] = _shift_in(vb_ref[g], vnb[:, g:g + 1])


def _sample_attn(layer, slope, sink_l, ht, caches, prev_outs):
    nb = ht.shape[0]
    assert caches[0].shape[-1] == WIN_A and caches[2].shape[-1] == WIN_B
    col = lambda hh: pl.BlockSpec((None, HEAD_DIM, hh), lambda i: (i, 0, 0))
    cache = lambda hh, n: pl.BlockSpec((None, None, hh, HEAD_DIM, n), lambda i: (layer, i, 0, 0, 0))
    small = pl.BlockSpec((N_HEADS, 1), lambda i: (0, 0))
    in_specs = [small, small, col(HT_COLS),
                cache(N_HEADS, WIN_A), cache(N_HEADS, WIN_A), cache(KV_B, WIN_B), cache(KV_B, WIN_B)]
    args = [slope, sink_l, ht, *caches]
    assert len(args) == N_SAMPLE_IN
    aliases = {}
    if prev_outs is not None:
        in_specs += [pl.BlockSpec(memory_space=pl.ANY)] * 4
        args += list(prev_outs)
        aliases = {N_SAMPLE_IN + k: 1 + k for k in range(4)}
    return pl.pallas_call(
        _sample_body,
        grid=(nb,),
        in_specs=in_specs,
        out_specs=[col(2 * N_HEADS), cache(N_HEADS, WIN_A), cache(N_HEADS, WIN_A),
                   cache(KV_B, WIN_B), cache(KV_B, WIN_B)],
        out_shape=[jax.ShapeDtypeStruct((nb, HEAD_DIM, 2 * N_HEADS), F32)] + [
            jax.ShapeDtypeStruct(c.shape, F32) for c in caches],
        input_output_aliases=aliases,
        compiler_params=_arb(1),
        name="sample_attn",
    )(*args)


P_TR = 512
P_SBK = 256
P_TM = 256


def kernel(x_prompt, x_sample, cache_a_k, cache_a_v, cache_b_k, cache_b_v, c_prompt, c_sample, w_ada, b_ada, w_in, sinks_b, gain_a, gain_b, w_out, ln1_g, ln1_b, ln2_g, ln2_b, router_w, router_b, w_gate, w_up, w_down):
    nbatch, seq, _ = x_prompt.shape
    nsamp = x_sample.shape[0]
    t_p = nbatch * seq
    slopes = jnp.asarray(2.0 ** (-8.0 * np.arange(1, N_HEADS + 1) / N_HEADS), dtype=F32)
    slopes_col = slopes.reshape(N_HEADS, 1)

    perm = np.array([g * EPG + j for j in range(EPG) for g in range(N_GROUPS)])
    rw_t = router_w.T[perm]
    rb = router_b[perm].reshape(N_EXPERTS, 1)

    pad = (-(nbatch + nsamp)) % SUBLANES
    c_all = jnp.concatenate([c_prompt, c_sample, jnp.zeros((pad, D_MODEL), F32)], axis=0)
    mod_all = _ada(c_all, w_ada, b_ada)

    to_lanes = lambda c: jnp.transpose(c, (0, 1, 3, 4, 2))
    from_lanes = lambda c: jnp.transpose(c, (0, 1, 4, 2, 3))
    caches = [to_lanes(c) for c in (cache_a_k, cache_a_v, cache_b_k, cache_b_v)]
    new_caches = None

    xp = x_prompt.reshape(t_p, D_MODEL)
    xs = x_sample.reshape(nsamp, D_MODEL)
    pak, pav, pbk, pbv = [], [], [], []
    kb0 = 4 * A_WIDTH
    for l in range(DEPTH):
        mod_p = mod_all[l, :nbatch].reshape(nbatch * 6, 1, D_MODEL)
        mod_s = mod_all[l, nbatch:nbatch + nsamp].reshape(nsamp, 6, D_MODEL).transpose(1, 0, 2)
        ga, gb = gain_a[l].reshape(1, A_WIDTH), gain_b[l].reshape(1, A_WIDTH)
        l1g, l1b = ln1_g[l].reshape(1, D_MODEL), ln1_b[l].reshape(1, D_MODEL)
        l2g, l2b = ln2_g[l].reshape(1, D_MODEL), ln2_b[l].reshape(1, D_MODEL)

        h = _proj(xp, mod_p, w_in[l], P_TR, seq // P_TR)
        oa = _attn_a(h, slopes, nbatch, seq)
        ob = _attn_b(h, slopes, sinks_b[l], nbatch, seq)
        x1, u2, tail, lp, cnt = _merge(oa, ob, xp, mod_p, ga, gb, w_out[l], l1g, l1b, rw_t, rb, P_SBK,
                                       seq // P_SBK, P_SBK)
        xp = _moe_layer(u2, tail, lp, cnt, x1, mod_p, w_gate, w_up, w_down, l2g, l2b, l, P_SBK, seq // P_SBK, P_TM)
        h3 = h.reshape(nbatch, seq, IN_COLS)
        pak.append(h3[:, seq - WIN_A:, A_WIDTH:2 * A_WIDTH].reshape(nbatch, WIN_A, N_HEADS, HEAD_DIM))
        pav.append(h3[:, seq - WIN_A:, 2 * A_WIDTH:3 * A_WIDTH].reshape(nbatch, WIN_A, N_HEADS, HEAD_DIM))
        pbk.append(h3[:, seq - WIN_B:, kb0:kb0 + LANES].reshape(nbatch, WIN_B, KV_B, HEAD_DIM))
        pbv.append(h3[:, seq - WIN_B:, kb0 + LANES:kb0 + 2 * LANES].reshape(nbatch, WIN_B, KV_B, HEAD_DIM))

        hs = _proj(xs, mod_s, w_in[l], nsamp, 1)
        ht = jnp.transpose(hs.reshape(nsamp, HT_COLS, HEAD_DIM), (0, 2, 1))
        os_t, *new_caches = _sample_attn(l, slopes_col, sinks_b[l].reshape(N_HEADS, 1), ht, caches, new_caches)
        os = jnp.transpose(os_t, (0, 2, 1)).reshape(nsamp, 2 * A_WIDTH)
        oas, obs = os[:, :A_WIDTH], os[:, A_WIDTH:]
        x1s, u2s, tails, lps, cnts = _merge(oas, obs, xs, mod_s, ga, gb, w_out[l], l1g, l1b, rw_t, rb, nsamp, 1,
                                            nsamp)
        xs = _moe_layer(u2s, tails, lps, cnts, x1s, mod_s, w_gate, w_up, w_down, l2g, l2b, l, nsamp, 1, nsamp)

    sak, sav, sbk, sbv = [from_lanes(c) for c in new_caches]
    return (xp.reshape(nbatch, seq, D_MODEL), xs.reshape(nsamp, 1, D_MODEL),
            jnp.stack(pak), jnp.stack(pav), jnp.stack(pbk), jnp.stack(pbv), sak, sav, sbk, sbv)
```

```python
import functools

import jax
import jax.numpy as jnp
import numpy as np
from jax import lax
from jax.experimental import pallas as pl
from jax.experimental.pallas import tpu as pltpu

F32 = jnp.float32
BF16 = jnp.bfloat16
I32 = jnp.int32

D_MODEL = 1024
DEPTH = 4
HEAD_DIM = 64
N_HEADS = 8
KV_B = 2
G_B = N_HEADS // KV_B
DILATIONS = ((128, 1), (512, 4), (2048, 16))
BAND = 128
WIN_A = 2048
WIN_B = 128
N_EXPERTS = 32
N_GROUPS = 8
EPG = N_EXPERTS // N_GROUPS
D_EXPERT = D_MODEL // 4
ALPHA = (2.0 * DEPTH) ** 0.25
LN_EPS = 1e-5
NEG = -1e30
ATTN_SCALE = HEAD_DIM ** -0.5
A_WIDTH = N_HEADS * HEAD_DIM
IN_COLS = 3 * A_WIDTH + A_WIDTH + 2 * KV_B * HEAD_DIM

LANES = 128
SUBLANES = 8
SUP = BAND * 16
Q4 = SUP // 4
UNROLL = 16
UNROLL_B = 8
TAIL = LANES
ROW_W = D_MODEL + TAIL


def _silu(x):
    return x * jax.nn.sigmoid(x)


def _layer_norm(z, g, b):
    mu = jnp.mean(z, axis=-1, keepdims=True)
    zc = z - mu
    var = jnp.mean(zc * zc, axis=-1, keepdims=True)
    return zc * lax.rsqrt(var + LN_EPS) * g + b


def _rms_norm(x, g):
    return x * lax.rsqrt(jnp.mean(x * x, axis=-1, keepdims=True) + LN_EPS) * g


def _arb(n):
    return pltpu.CompilerParams(dimension_semantics=("arbitrary",) * n)


ADA_TN = 512


def _ada_body(c_ref, w_ref, b_ref, o_ref):
    a = _silu(c_ref[...]).astype(BF16)
    o_ref[...] = jnp.dot(a, w_ref[...].astype(BF16), preferred_element_type=F32) + b_ref[...]


def _ada(c_all, w_ada, b_ada):
    rows = c_all.shape[0]
    ncol = w_ada.shape[2]
    return pl.pallas_call(
        _ada_body,
        grid=(DEPTH, ncol // ADA_TN),
        in_specs=[
            pl.BlockSpec((rows, D_MODEL), lambda l, j: (0, 0)),
            pl.BlockSpec((None, D_MODEL, ADA_TN), lambda l, j: (l, 0, j)),
            pl.BlockSpec((None, 1, ADA_TN), lambda l, j: (l, 0, j)),
        ],
        out_specs=pl.BlockSpec((None, rows, ADA_TN), lambda l, j: (l, 0, j)),
        out_shape=jax.ShapeDtypeStruct((DEPTH, rows, ncol), F32),
        compiler_params=_arb(2),
        name="ada",
    )(c_all, w_ada, b_ada.reshape(DEPTH, 1, ncol))


def _mod_spec(k, tiles_per_batch, r):
    return pl.BlockSpec((None, r, D_MODEL), lambda i: ((i // tiles_per_batch) * 6 + k, 0, 0))


def _proj_body(x_ref, sh_ref, sc_ref, w_ref, o_ref, wb):
    @pl.when(pl.program_id(0) == 0)
    def _():
        wb[...] = w_ref[...].astype(BF16)

    u = x_ref[...] * (1.0 + sc_ref[...]) + sh_ref[...]
    o_ref[...] = jnp.dot(u.astype(BF16), wb[...], preferred_element_type=F32)


def _proj(x, mod, w_in_l, tr, tiles_per_batch):
    t = x.shape[0]
    r = mod.shape[1]
    return pl.pallas_call(
        _proj_body,
        grid=(t // tr,),
        in_specs=[
            pl.BlockSpec((tr, D_MODEL), lambda i: (i, 0)),
            _mod_spec(0, tiles_per_batch, r),
            _mod_spec(1, tiles_per_batch, r),
            pl.BlockSpec((D_MODEL, IN_COLS), lambda i: (0, 0)),
        ],
        out_specs=pl.BlockSpec((tr, IN_COLS), lambda i: (i, 0)),
        out_shape=jax.ShapeDtypeStruct((t, IN_COLS), F32),
        scratch_shapes=[pltpu.VMEM((D_MODEL, IN_COLS), BF16)],
        compiler_params=_arb(1),
        name="proj",
    )(x, mod, mod, w_in_l)


def _band_unit(q, kw, vw, bias):
    lane = lax.broadcasted_iota(I32, (BAND, LANES), 1)
    qs = q * ATTN_SCALE
    q2 = jnp.concatenate([jnp.where(lane < HEAD_DIM, qs, 0.0), jnp.where(lane >= HEAD_DIM, qs, 0.0)], axis=0)
    s = lax.dot_general(q2.astype(BF16), kw.astype(BF16), (((1,), (1,)), ((), ())), preferred_element_type=F32)
    s = s + bias
    m = jnp.max(s, axis=1, keepdims=True)
    pe = jnp.exp(s - m).astype(BF16)
    vx = jnp.concatenate([vw.astype(BF16), jnp.ones((2 * BAND, LANES), BF16)], axis=1)
    pvx = jnp.dot(pe, vx, preferred_element_type=F32)
    return pvx[:, :LANES], m, pvx[:, LANES:]


def _pair_merge(a):
    lane = lax.broadcasted_iota(I32, (BAND, LANES), 1)
    return jnp.where(lane < HEAD_DIM, jnp.broadcast_to(a[:BAND], (BAND, LANES)),
                     jnp.broadcast_to(a[BAND:], (BAND, LANES)))


def _make_bias(slope0, slope1, d, first):
    row = lax.broadcasted_iota(I32, (2 * BAND, 2 * BAND), 0)
    col = lax.broadcasted_iota(I32, (2 * BAND, 2 * BAND), 1)
    i = jnp.where(row >= BAND, row - BAND, row)
    dist = i + BAND - col
    valid = (dist >= 0) & (dist <= BAND)
    if first:
        valid = valid & (col >= BAND)
    slope = jnp.where(row >= BAND, slope1, slope0)
    return jnp.where(valid, -(slope * (dist * d).astype(F32)), NEG)


def _attn_a_body(slopes_ref, q_ref, k_ref, v_ref, o_ref, qa, ka, va, ktail, vtail, acc, ms, ls, bias):
    p = pl.program_id(1)
    st = pl.program_id(2)
    cur = st % 2
    prv = 1 - cur

    @pl.when(st == 0)
    def _():
        ka[1] = jnp.zeros((4, Q4, LANES), F32)
        va[1] = jnp.zeros((4, Q4, LANES), F32)
        ktail[...] = jnp.zeros((BAND, LANES), F32)
        vtail[...] = jnp.zeros((BAND, LANES), F32)
        s0 = slopes_ref[2 * p]
        s1 = slopes_ref[2 * p + 1]
        for ci, (_, d) in enumerate(DILATIONS):
            bias[ci, 0] = _make_bias(s0, s1, d, False)
            bias[ci, 1] = _make_bias(s0, s1, d, True)

    for c in range(4):
        qa[c] = q_ref[pl.ds(c, Q4, stride=4), :]
        ka[cur, c] = k_ref[pl.ds(c, Q4, stride=4), :]
        va[cur, c] = v_ref[pl.ds(c, Q4, stride=4), :]

    first0 = (st == 0).astype(I32)
    n_groups = SUP // BAND // UNROLL

    def store(ci, idx, pv, m, l):
        acc[ci, idx, :] = _pair_merge(pv)
        ms[ci, idx, :] = _pair_merge(m)
        ls[ci, idx, :] = _pair_merge(l)

    pv, m, l = _band_unit(q_ref[0:BAND, :], jnp.concatenate([ktail[...], k_ref[0:BAND, :]], axis=0),
                          jnp.concatenate([vtail[...], v_ref[0:BAND, :]], axis=0), bias[0, first0])
    store(0, pl.ds(0, BAND), pv, m, l)

    def unit1(n):
        qs = pl.multiple_of(n * BAND, BAND)
        ks = pl.multiple_of(n * BAND - BAND, BAND)
        pv, m, l = _band_unit(q_ref[pl.ds(qs, BAND), :], k_ref[pl.ds(ks, 2 * BAND), :],
                              v_ref[pl.ds(ks, 2 * BAND), :], bias[0, 0])
        store(0, pl.ds(qs, BAND), pv, m, l)

    for n in range(1, UNROLL):
        unit1(n)

    def group1(g, carry):
        for j in range(UNROLL):
            unit1(g * UNROLL + j)
        return carry
    lax.fori_loop(1, n_groups, group1, 0)
    ktail[...] = k_ref[SUP - BAND:SUP, :]
    vtail[...] = v_ref[SUP - BAND:SUP, :]

    def unit4(u, r, first):
        qs = pl.multiple_of(u * BAND, BAND)
        if first:
            kw = jnp.concatenate([ka[prv, r, Q4 - BAND:Q4, :], ka[cur, r, 0:BAND, :]], axis=0)
            vw = jnp.concatenate([va[prv, r, Q4 - BAND:Q4, :], va[cur, r, 0:BAND, :]], axis=0)
            b = bias[1, first0]
        else:
            kw = ka[cur, r, pl.ds(qs - BAND, 2 * BAND), :]
            vw = va[cur, r, pl.ds(qs - BAND, 2 * BAND), :]
            b = bias[1, 0]
        pv, m, l = _band_unit(qa[r, pl.ds(qs, BAND), :], kw, vw, b)
        store(1, pl.ds(r * Q4 + qs, BAND), pv, m, l)

    for n in range(UNROLL):
        unit4(n // 4, n % 4, n < 4)

    def group4(g, carry):
        for j in range(UNROLL):
            unit4(g * (UNROLL // 4) + j // 4, j % 4, False)
        return carry
    lax.fori_loop(1, n_groups, group4, 0)

    def group16(g, carry):
        for j in range(UNROLL):
            c = g * (UNROLL // 4) + j // 4
            a = j % 4
            idx = pl.ds(a, BAND, stride=4)
            kw = jnp.concatenate([ka[prv, c, idx, :], ka[cur, c, idx, :]], axis=0)
            vw = jnp.concatenate([va[prv, c, idx, :], va[cur, c, idx, :]], axis=0)
            pv, m, l = _band_unit(qa[c, idx, :], kw, vw, bias[2, first0])
            store(2, pl.ds(c * Q4 + a, BAND, stride=4), pv, m, l)
        return carry
    lax.fori_loop(0, n_groups, group16, 0)

    def fin(i, carry):
        c = i // 4
        rows = pl.ds(pl.multiple_of(i * BAND, BAND), BAND)
        tok = pl.ds(c + 4 * BAND * (i - 4 * c), BAND, stride=4)
        m0, m1, m2 = ms[0, tok, :], ms[1, rows, :], ms[2, rows, :]
        mm = jnp.maximum(jnp.maximum(m0, m1), m2)
        w0, w1, w2 = jnp.exp(m0 - mm), jnp.exp(m1 - mm), jnp.exp(m2 - mm)
        den = w0 * ls[0, tok, :] + w1 * ls[1, rows, :] + w2 * ls[2, rows, :]
        num = w0 * acc[0, tok, :] + w1 * acc[1, rows, :] + w2 * acc[2, rows, :]
        o_ref[tok, :] = num / den
        return carry
    lax.fori_loop(0, SUP // BAND, fin, 0)


def _attn_a(h, slopes, nbatch, seq):
    assert UNROLL % 4 == 0 and (SUP // BAND) % UNROLL == 0 and seq % SUP == 0
    nst = seq // SUP
    npair = N_HEADS // 2

    def col(c0):
        return lambda b, p, st: (b * nst + st, c0 + p)

    return pl.pallas_call(
        _attn_a_body,
        grid=(nbatch, npair, nst),
        in_specs=[
            pl.BlockSpec(memory_space=pltpu.SMEM),
            pl.BlockSpec((SUP, LANES), col(0)),
            pl.BlockSpec((SUP, LANES), col(npair)),
            pl.BlockSpec((SUP, LANES), col(2 * npair)),
        ],
        out_specs=pl.BlockSpec((SUP, LANES), col(0)),
        out_shape=jax.ShapeDtypeStruct((nbatch * seq, A_WIDTH), F32),
        scratch_shapes=[
            pltpu.VMEM((4, Q4, LANES), F32),
            pltpu.VMEM((2, 4, Q4, LANES), F32),
            pltpu.VMEM((2, 4, Q4, LANES), F32),
            pltpu.VMEM((BAND, LANES), F32),
            pltpu.VMEM((BAND, LANES), F32),
            pltpu.VMEM((3, SUP, LANES), F32),
            pltpu.VMEM((3, SUP, LANES), F32),
            pltpu.VMEM((3, SUP, LANES), F32),
            pltpu.VMEM((3, 2, 2 * BAND, 2 * BAND), F32),
        ],
        compiler_params=_arb(3),
        name="attn_a",
    )(slopes, h, h, h)


def _attn_b_body(slopes_ref, sinks_ref, q_ref, k_ref, v_ref, o_ref, k2, v2, bias):
    p = pl.program_id(1)
    st = pl.program_id(2)

    @pl.when(st == 0)
    def _():
        k2[0:BAND, :] = jnp.zeros((BAND, LANES), F32)
        v2[0:BAND, :] = jnp.zeros((BAND, LANES), F32)
        s0 = slopes_ref[2 * p]
        s1 = slopes_ref[2 * p + 1]
        bias[0] = _make_bias(s0, s1, 1, False)
        bias[1] = _make_bias(s0, s1, 1, True)

    @pl.when(st > 0)
    def _():
        k2[0:BAND, :] = k2[SUP:SUP + BAND, :]
        v2[0:BAND, :] = v2[SUP:SUP + BAND, :]

    lane = lax.broadcasted_iota(I32, (SUP, LANES), 1)
    keep = (lane < HEAD_DIM) == (p < G_B // 2)
    k2[BAND:BAND + SUP, :] = jnp.where(keep, k_ref[...], pltpu.roll(k_ref[...], HEAD_DIM, 1))
    v2[BAND:BAND + SUP, :] = jnp.where(keep, v_ref[...], pltpu.roll(v_ref[...], HEAD_DIM, 1))

    row = lax.broadcasted_iota(I32, (2 * BAND, 1), 0)
    sink = jnp.where(row >= BAND, sinks_ref[2 * p + 1], sinks_ref[2 * p])

    def group(g, carry):
        for j in range(UNROLL_B):
            n = g * UNROLL_B + j
            qstart = pl.multiple_of(n * BAND, BAND)
            first = jnp.logical_and(st == 0, n == 0).astype(I32)
            pv, m, l = _band_unit(q_ref[pl.ds(qstart, BAND), :], k2[pl.ds(qstart, 2 * BAND), :],
                                  v2[pl.ds(qstart, 2 * BAND), :], bias[first])
            m2 = jnp.maximum(m, sink)
            scale = jnp.exp(m - m2)
            den = l * scale + jnp.exp(sink - m2)
            o_ref[pl.ds(qstart, BAND), :] = _pair_merge(pv * scale / den)
        return carry
    lax.fori_loop(0, SUP // BAND // UNROLL_B, group, 0)


def _attn_b(h, slopes, sinks_l, nbatch, seq):
    nst = seq // SUP
    npair = N_HEADS // 2
    qcol0 = 3 * npair
    kcol = 4 * npair
    vcol = kcol + 1
    return pl.pallas_call(
        _attn_b_body,
        grid=(nbatch, npair, nst),
        in_specs=[
            pl.BlockSpec(memory_space=pltpu.SMEM),
            pl.BlockSpec(memory_space=pltpu.SMEM),
            pl.BlockSpec((SUP, LANES), lambda b, p, st: (b * nst + st, qcol0 + p)),
            pl.BlockSpec((SUP, LANES), lambda b, p, st: (b * nst + st, kcol)),
            pl.BlockSpec((SUP, LANES), lambda b, p, st: (b * nst + st, vcol)),
        ],
        out_specs=pl.BlockSpec((SUP, LANES), lambda b, p, st: (b * nst + st, p)),
        out_shape=jax.ShapeDtypeStruct((nbatch * seq, A_WIDTH), F32),
        scratch_shapes=[
            pltpu.VMEM((BAND + SUP, LANES), F32),
            pltpu.VMEM((BAND + SUP, LANES), F32),
            pltpu.VMEM((2, 2 * BAND, 2 * BAND), F32),
        ],
        compiler_params=_arb(3),
        name="attn_b",
    )(slopes, sinks_l, h, h, h)


def _route(logits_t, rb):
    aff = jax.nn.sigmoid(logits_t)
    sel = aff + rb
    a = [aff[SUBLANES * j:SUBLANES * (j + 1)] for j in range(EPG)]
    s = [sel[SUBLANES * j:SUBLANES * (j + 1)] for j in range(EPG)]
    hi1, lo1 = jnp.maximum(s[0], s[1]), jnp.minimum(s[0], s[1])
    hi2, lo2 = jnp.maximum(s[2], s[3]), jnp.minimum(s[2], s[3])
    top1 = jnp.maximum(hi1, hi2)
    top2 = jnp.maximum(jnp.minimum(hi1, hi2), jnp.maximum(lo1, lo2))
    gscore = top1 + top2
    giota = lax.broadcasted_iota(I32, gscore.shape, 0)
    gmax = jnp.max(gscore, axis=0, keepdims=True)
    gbest = jnp.min(jnp.where(gscore == gmax, giota, N_GROUPS), axis=0, keepdims=True)
    gmask = giota == gbest
    v = [jnp.sum(jnp.where(gmask, sj, 0.0), axis=0, keepdims=True) for sj in s]
    av = [jnp.sum(jnp.where(gmask, aj, 0.0), axis=0, keepdims=True) for aj in a]
    m1 = jnp.maximum(jnp.maximum(v[0], v[1]), jnp.maximum(v[2], v[3]))
    i1 = jnp.where(v[0] == m1, 0, jnp.where(v[1] == m1, 1, jnp.where(v[2] == m1, 2, 3)))
    w = [jnp.where(i1 == j, -jnp.inf, v[j]) for j in range(EPG)]
    m2 = jnp.maximum(jnp.maximum(w[0], w[1]), jnp.maximum(w[2], w[3]))
    i2 = jnp.where(w[0] == m2, 0, jnp.where(w[1] == m2, 1, jnp.where(w[2] == m2, 2, 3)))
    a1 = sum(jnp.where(i1 == j, av[j], 0.0) for j in range(EPG))
    a2 = sum(jnp.where(i2 == j, av[j], 0.0) for j in range(EPG))
    den = a1 + a2
    comb = [jnp.where(i1 == j, a1 / den, 0.0) + jnp.where(i2 == j, a2 / den, 0.0) for j in range(EPG)]
    return gbest, comb


RUN_ALIGN = SUBLANES
LP_LANE = EPG


def _round_up_i32(x, m):
    shift = int(np.log2(m))
    return ((x + (m - 1)) >> shift) << shift


def _group_starts(padded):
    giota = lax.broadcasted_iota(I32, padded.shape, 0)
    off = jnp.zeros(padded.shape, I32)
    for g in range(N_GROUPS - 1):
        off = off + jnp.where(giota > g, padded[g:g + 1, :], 0)
    return off


def _merge_body(oa_ref, ob_ref, x_ref, gt_ref, sh_ref, sc_ref, ga_ref, gb_ref, w_ref, lg_ref, lb_ref,
                rw_ref, rb_ref, x1_ref, u2_ref, tail_ref, lp_ref, cnt_ref, wb, tri, *, sbk):
    @pl.when(pl.program_id(0) == 0)
    def _():
        wb[...] = w_ref[...].astype(BF16)
        r_i = lax.broadcasted_iota(I32, (sbk, sbk), 0)
        c_i = lax.broadcasted_iota(I32, (sbk, sbk), 1)
        tri[...] = (r_i <= c_i).astype(BF16)

    def rows_of(ref, rows):
        return ref[...] if ref.shape[0] == 1 else ref[rows, :]

    for k in range(x_ref.shape[0] // sbk):
        rows = slice(k * sbk, (k + 1) * sbk)
        ya = _rms_norm(oa_ref[rows, :], ga_ref[...])
        yb = _rms_norm(ob_ref[rows, :], gb_ref[...])
        y = jnp.concatenate([ya, yb], axis=-1).astype(BF16)
        mix = jnp.dot(y, wb[...], preferred_element_type=F32)
        x1 = _layer_norm(ALPHA * x_ref[rows, :] + rows_of(gt_ref, rows) * mix, lg_ref[...], lb_ref[...])
        x1_ref[rows, :] = x1
        u2 = x1 * (1.0 + rows_of(sc_ref, rows)) + rows_of(sh_ref, rows)
        u2_ref[rows, :] = u2

        logits_t = lax.dot_general(rw_ref[...], u2, (((1,), (1,)), ((), ())), precision=lax.Precision.HIGHEST,
                                   preferred_element_type=F32)
        gbest, comb = _route(logits_t, rb_ref[...])
        onehot = lax.broadcasted_iota(I32, (N_GROUPS, sbk), 0) == gbest
        cnt = jnp.sum(onehot.astype(F32), axis=1, keepdims=True)
        starts = _group_starts(_round_up_i32(cnt.astype(I32), RUN_ALIGN)).astype(F32)
        inc = jnp.dot(onehot.astype(BF16), tri[...], preferred_element_type=F32)
        lp = jnp.sum(jnp.where(onehot, starts + inc - 1.0, 0.0), axis=0, keepdims=True)
        lp_ref[k] = lp
        cnt_ref[k] = jnp.broadcast_to(cnt, (N_GROUPS, LANES))
        row = lax.broadcasted_iota(I32, (TAIL, sbk), 0)
        slab = jnp.where(row == LP_LANE, lp, 0.0)
        for j in range(EPG):
            slab = jnp.where(row == j, comb[j], slab)
        tail_ref[rows, :] = slab.T


def _merge(oa, ob, x, mod, ga, gb, w_out_l, lg, lb, rw_t, rb, tr, tiles_per_batch, sbk):
    t = x.shape[0]
    r = mod.shape[1]
    nt = t // tr
    nsb = tr // sbk
    row = lambda i: (i, 0)
    fixed = lambda i: (0, 0)
    return pl.pallas_call(
        functools.partial(_merge_body, sbk=sbk),
        grid=(nt,),
        in_specs=[
            pl.BlockSpec((tr, A_WIDTH), row),
            pl.BlockSpec((tr, A_WIDTH), row),
            pl.BlockSpec((tr, D_MODEL), row),
            _mod_spec(2, tiles_per_batch, r),
            _mod_spec(3, tiles_per_batch, r),
            _mod_spec(4, tiles_per_batch, r),
            pl.BlockSpec((1, A_WIDTH), fixed),
            pl.BlockSpec((1, A_WIDTH), fixed),
            pl.BlockSpec((D_MODEL, D_MODEL), fixed),
            pl.BlockSpec((1, D_MODEL), fixed),
            pl.BlockSpec((1, D_MODEL), fixed),
            pl.BlockSpec((N_EXPERTS, D_MODEL), fixed),
            pl.BlockSpec((N_EXPERTS, 1), fixed),
        ],
        out_specs=[
            pl.BlockSpec((tr, D_MODEL), row),
            pl.BlockSpec((tr, D_MODEL), row),
            pl.BlockSpec((tr, TAIL), row),
            pl.BlockSpec((nsb, 1, sbk), lambda i: (i, 0, 0)),
            pl.BlockSpec((nsb, N_GROUPS, LANES), lambda i: (i, 0, 0)),
        ],
        out_shape=[
            jax.ShapeDtypeStruct((t, D_MODEL), F32),
            jax.ShapeDtypeStruct((t, D_MODEL), F32),
            jax.ShapeDtypeStruct((t, TAIL), F32),
            jax.ShapeDtypeStruct((t // sbk, 1, sbk), F32),
            jax.ShapeDtypeStruct((t // sbk, N_GROUPS, LANES), F32),
        ],
        scratch_shapes=[pltpu.VMEM((D_MODEL, D_MODEL), BF16), pltpu.VMEM((sbk, sbk), BF16)],
        compiler_params=_arb(1),
        name="merge",
    )(oa, ob, x, mod, mod, mod, ga, gb, w_out_l, lg, lb, rw_t, rb)


PLAN_SIZE, PLAN_LOCAL, PLAN_GLOBAL = 0, N_GROUPS, 2 * N_GROUPS


def _plan_body(cnt_ref, plan_ref, meta_ref, run_off, *, tm):
    nt = cnt_ref.shape[0]

    def sizes(i):
        return _round_up_i32(cnt_ref[i].astype(I32), RUN_ALIGN)

    def first(i, total):
        run_off[i] = total
        return total + sizes(i)
    totals = lax.fori_loop(0, nt, first, jnp.zeros((N_GROUPS, LANES), I32))
    padded = _round_up_i32(totals, tm)
    goff = _group_starts(padded)
    ends = goff + padded
    giota = lax.broadcasted_iota(I32, (N_GROUPS, LANES), 0)
    lane = lax.broadcasted_iota(I32, (N_GROUPS, LANES), 1)
    tile_group = jnp.sum((ends <= lane * tm).astype(F32), axis=0, keepdims=True).astype(I32)
    tile_group = jnp.minimum(tile_group, N_GROUPS - 1)
    n_used = ends[N_GROUPS - 1:N_GROUPS, :] >> int(np.log2(tm))
    meta_ref[...] = jnp.where(giota == 0, tile_group, jnp.where(giota == 1, n_used, 0))

    def on_lanes(v):
        picked = jnp.where(giota == (lane & (N_GROUPS - 1)), v, 0).astype(F32)
        return jnp.sum(picked, axis=0, keepdims=True).astype(I32)

    lane1 = lax.broadcasted_iota(I32, (1, LANES), 1)

    def second(i, carry):
        sz = sizes(i)
        row = jnp.where(lane1 < PLAN_LOCAL, on_lanes(sz),
                        jnp.where(lane1 < PLAN_GLOBAL, on_lanes(_group_starts(sz)), on_lanes(goff + run_off[i])))
        plan_ref[i] = row
        return carry
    lax.fori_loop(0, nt, second, 0)


def _plan(cnt, tm):
    nt = cnt.shape[0]
    return pl.pallas_call(
        functools.partial(_plan_body, tm=tm),
        out_shape=[jax.ShapeDtypeStruct((nt, 1, LANES), I32), jax.ShapeDtypeStruct((N_GROUPS, LANES), I32)],
        scratch_shapes=[pltpu.VMEM((nt, N_GROUPS, LANES), I32)],
        name="plan",
    )(cnt)


def _run_copies(plan_ref, sbk, make):
    out = []
    sizes = [s for s in (256, 128, 64, 32, 16, 8) if s <= sbk]
    for g in range(N_GROUPS):
        n = plan_ref[0, PLAN_SIZE + g]
        loc = plan_ref[0, PLAN_LOCAL + g]
        glo = plan_ref[0, PLAN_GLOBAL + g]
        done = jnp.int32(0)
        for s in sizes:
            cond = (n & s) != 0
            out.append((cond, make(pl.multiple_of(loc + done, RUN_ALIGN), pl.multiple_of(glo + done, RUN_ALIGN), s)))
            done = done + jnp.where(cond, s, 0)
    return out


def _start(copies):
    for cond, cp in copies:
        pl.when(cond)(cp.start)


def _wait(copies):
    for cond, cp in copies:
        pl.when(cond)(cp.wait)


def _split_bf16(x):
    hi = x.astype(BF16)
    return hi, (x - hi.astype(F32)).astype(BF16)


def _scatter_body(plan_ref, plan_prev_ref, lp_ref, u_ref, tail_ref, dst_in, dst_ref, buf, sem):
    del dst_in
    i = pl.program_id(0)
    slot = i % 2
    sbk = u_ref.shape[0]
    lr = buf.shape[1]

    def copies(plan, s):
        return _run_copies(plan, sbk, lambda loc, glo, n: pltpu.make_async_copy(
            buf.at[s, pl.ds(loc, n), :], dst_ref.at[pl.ds(glo, n), :], sem.at[s]))

    perm = (lax.broadcasted_iota(I32, (lr, sbk), 0) == lp_ref[...].astype(I32)).astype(BF16)
    t_hi, t_lo = _split_bf16(tail_ref[...])
    x = jnp.concatenate([u_ref[...].astype(BF16), t_hi, t_lo], axis=1)
    y = jnp.dot(perm, x, preferred_element_type=F32)
    buf[slot, :, 0:D_MODEL] = y[:, 0:D_MODEL]
    buf[slot, :, D_MODEL:ROW_W] = y[:, D_MODEL:ROW_W] + y[:, ROW_W:ROW_W + TAIL]
    _start(copies(plan_ref, slot))

    @pl.when(i > 0)
    def _():
        _wait(copies(plan_prev_ref, 1 - slot))

    @pl.when(i == pl.num_programs(0) - 1)
    def _():
        _wait(copies(plan_ref, slot))


def _scatter_rows(plan, lp, u2, tail, n_sorted, sbk):
    t = u2.shape[0]
    dst = jnp.zeros((n_sorted, ROW_W), F32)
    return pl.pallas_call(
        _scatter_body,
        grid=(t // sbk,),
        in_specs=[
            pl.BlockSpec((None, 1, LANES), lambda i: (i, 0, 0), memory_space=pltpu.SMEM),
            pl.BlockSpec((None, 1, LANES), lambda i: (jnp.maximum(i - 1, 0), 0, 0), memory_space=pltpu.SMEM),
            pl.BlockSpec((None, 1, sbk), lambda i: (i, 0, 0)),
            pl.BlockSpec((sbk, D_MODEL), lambda i: (i, 0)),
            pl.BlockSpec((sbk, TAIL), lambda i: (i, 0)),
            pl.BlockSpec(memory_space=pl.ANY),
        ],
        out_specs=pl.BlockSpec(memory_space=pl.ANY),
        out_shape=jax.ShapeDtypeStruct((n_sorted, ROW_W), F32),
        scratch_shapes=[pltpu.VMEM((2, sbk + LANES, ROW_W), F32), pltpu.SemaphoreType.DMA((2,))],
        input_output_aliases={5: 0},
        compiler_params=_arb(1),
        name="scatter_rows",
    )(plan, plan, lp, u2, tail, dst)


def _combine_body(plan_ref, plan_next_ref, ys_ref, tail_ref, x1_ref, gt_ref, lg_ref, lb_ref, o_ref, buf, sem):
    i = pl.program_id(0)
    slot = i % 2
    sbk = x1_ref.shape[0]
    lr = buf.shape[1]

    def copies(plan, s):
        return _run_copies(plan, sbk, lambda loc, glo, n: pltpu.make_async_copy(
            ys_ref.at[pl.ds(glo, n), :], buf.at[s, pl.ds(loc, n), :], sem.at[s]))

    @pl.when(i == 0)
    def _():
        _start(copies(plan_ref, slot))

    @pl.when(i < pl.num_programs(0) - 1)
    def _():
        _start(copies(plan_next_ref, 1 - slot))

    _wait(copies(plan_ref, slot))
    n_rows = plan_ref[0, PLAN_LOCAL + N_GROUPS - 1] + plan_ref[0, PLAN_SIZE + N_GROUPS - 1]
    ys = jnp.where(lax.broadcasted_iota(I32, (lr, 1), 0) < n_rows, buf[slot], 0.0)
    lp = tail_ref[:, LP_LANE:LP_LANE + 1].astype(I32)
    perm_t = (lax.broadcasted_iota(I32, (sbk, lr), 1) == lp).astype(BF16)
    y_hi, y_lo = _split_bf16(ys)
    y = jnp.dot(perm_t, y_hi, preferred_element_type=F32) + jnp.dot(perm_t, y_lo, preferred_element_type=F32)
    o_ref[...] = _layer_norm(ALPHA * x1_ref[...] + gt_ref[...] * y, lg_ref[...], lb_ref[...])


def _combine(plan, ys, tail, x1, mod, lg, lb, sbk, tiles_per_batch):
    t = x1.shape[0]
    r = mod.shape[1]
    nt = t // sbk
    return pl.pallas_call(
        _combine_body,
        grid=(nt,),
        in_specs=[
            pl.BlockSpec((None, 1, LANES), lambda i: (i, 0, 0), memory_space=pltpu.SMEM),
            pl.BlockSpec((None, 1, LANES), lambda i: (jnp.minimum(i + 1, nt - 1), 0, 0), memory_space=pltpu.SMEM),
            pl.BlockSpec(memory_space=pl.ANY),
            pl.BlockSpec((sbk, TAIL), lambda i: (i, 0)),
            pl.BlockSpec((sbk, D_MODEL), lambda i: (i, 0)),
            _mod_spec(5, tiles_per_batch, r),
            pl.BlockSpec((1, D_MODEL), lambda i: (0, 0)),
            pl.BlockSpec((1, D_MODEL), lambda i: (0, 0)),
        ],
        out_specs=pl.BlockSpec((sbk, D_MODEL), lambda i: (i, 0)),
        out_shape=jax.ShapeDtypeStruct((t, D_MODEL), F32),
        scratch_shapes=[pltpu.VMEM((2, sbk + LANES, D_MODEL), F32), pltpu.SemaphoreType.DMA((2,))],
        compiler_params=_arb(1),
        name="combine",
    )(plan, plan, ys, tail, x1, mod, lg, lb)


def _moe_body(tg_ref, nu_ref, xs_ref, wg_ref, wu_ref, wd_ref, ys_ref, wgb, wub, wdb):
    i = pl.program_id(0)
    prev = tg_ref[jnp.maximum(i - 1, 0)]

    @pl.when(jnp.logical_or(i == 0, tg_ref[i] != prev))
    def _():
        wgb[...] = wg_ref[...].astype(BF16)
        wub[...] = wu_ref[...].astype(BF16)
        wdb[...] = wd_ref[...].astype(BF16)

    @pl.when(i < nu_ref[0])
    def _():
        x = xs_ref[:, 0:D_MODEL].astype(BF16)
        comb = xs_ref[:, D_MODEL:ROW_W]
        y = jnp.zeros((x.shape[0], D_MODEL), F32)
        for e in range(EPG):
            hg = jnp.dot(x, wgb[e], preferred_element_type=F32)
            hu = jnp.dot(x, wub[e], preferred_element_type=F32)
            act = _silu(hg) * hu * comb[:, e:e + 1]
            y = y + jnp.dot(act.astype(BF16), wdb[e], preferred_element_type=F32)
        ys_ref[...] = y

    @pl.when(i >= nu_ref[0])
    def _():
        ys_ref[...] = jnp.zeros(ys_ref.shape, F32)


def _moe(tile_group, n_used, xs, w_gate, w_up, w_down, layer, tm):
    n_sorted = xs.shape[0]
    nt = n_sorted // tm
    grid_spec = pltpu.PrefetchScalarGridSpec(
        num_scalar_prefetch=2,
        grid=(nt,),
        in_specs=[
            pl.BlockSpec((tm, ROW_W), lambda i, tg, nu: (i, 0)),
            pl.BlockSpec((None, EPG, D_MODEL, D_EXPERT), lambda i, tg, nu: (layer, tg[i], 0, 0)),
            pl.BlockSpec((None, EPG, D_MODEL, D_EXPERT), lambda i, tg, nu: (layer, tg[i], 0, 0)),
            pl.BlockSpec((None, EPG, D_EXPERT, D_MODEL), lambda i, tg, nu: (layer, tg[i], 0, 0)),
        ],
        out_specs=pl.BlockSpec((tm, D_MODEL), lambda i, tg, nu: (i, 0)),
        scratch_shapes=[
            pltpu.VMEM((EPG, D_MODEL, D_EXPERT), BF16),
            pltpu.VMEM((EPG, D_MODEL, D_EXPERT), BF16),
            pltpu.VMEM((EPG, D_EXPERT, D_MODEL), BF16),
        ],
    )
    return pl.pallas_call(
        _moe_body,
        grid_spec=grid_spec,
        out_shape=jax.ShapeDtypeStruct((n_sorted, D_MODEL), F32),
        compiler_params=_arb(1),
        name="moe",
    )(tile_group, n_used, xs, w_gate, w_up, w_down)


def _moe_layer(u2, tail, lp, cnt, x1, mod, w_gate, w_up, w_down, lg, lb, layer, sbk, tiles_per_batch, tm):
    t = x1.shape[0]
    nt = t // sbk
    n_tiles = -(-(t + nt * N_GROUPS * (RUN_ALIGN - 1)) // tm) + N_GROUPS
    assert n_tiles <= LANES
    plan, meta = _plan(cnt, tm)
    xs = _scatter_rows(plan, lp, u2, tail, n_tiles * tm, sbk)
    ys = _moe(meta[0, :n_tiles], meta[1, :1], xs, w_gate, w_up, w_down, layer, tm)
    return _combine(plan, ys, tail, x1, mod, lg, lb, sbk, tiles_per_batch)


def _shift_in(x, new_col):
    n = x.shape[-1]
    lane = lax.broadcasted_iota(I32, x.shape, x.ndim - 1)
    return jnp.where(lane == n - 1, new_col, pltpu.roll(x, n - 1, x.ndim - 1))


def _decode(q_t, k_of, v_of, kn_t, vn_t, head_kv, slope, mult, mult_new, sink):
    n = mult.shape[1]
    rows, rows_new = [], []
    for h in range(N_HEADS):
        g = head_kv[h]
        qh = q_t[:, h:h + 1]
        rows.append(jnp.sum(qh * k_of(g), axis=0, keepdims=True))
        rows_new.append(jnp.sum(qh * kn_t[:, g:g + 1], axis=0, keepdims=True))
    pos = lax.broadcasted_iota(I32, (1, n), 1)
    s = jnp.concatenate(rows, axis=0) * ATTN_SCALE - slope * (n - pos).astype(F32)
    s = jnp.where(mult > 0.0, s, NEG)
    s_new = jnp.concatenate(rows_new, axis=0) * ATTN_SCALE
    m = jnp.maximum(jnp.max(s, axis=1, keepdims=True), s_new)
    if sink is not None:
        m = jnp.maximum(m, sink)
    p = mult * jnp.exp(s - m)
    p_new = mult_new * jnp.exp(s_new - m)
    den = jnp.sum(p, axis=1, keepdims=True) + p_new
    if sink is not None:
        den = den + jnp.exp(sink - m)
    w = p / den
    w_new = p_new / den
    cols = []
    for h in range(N_HEADS):
        g = head_kv[h]
        cols.append(jnp.sum(w[h:h + 1, :] * v_of(g), axis=1, keepdims=True) + w_new[h:h + 1, :] * vn_t[:, g:g + 1])
    return jnp.concatenate(cols, axis=1)


def _dilated_multiplicity(n):
    pos = lax.broadcasted_iota(I32, (1, n), 1)
    dist = n - pos
    mult = jnp.zeros((1, n), F32)
    for (w, d) in DILATIONS:
        mult = mult + jnp.where((dist <= w) & ((dist & (d - 1)) == 0), 1.0, 0.0)
    return mult


N_SAMPLE_IN = 7
HT_COLS = IN_COLS // HEAD_DIM


def _sample_body(*refs):
    slope_ref, sink_ref, ht_ref, ka_ref, va_ref, kb_ref, vb_ref = refs[:N_SAMPLE_IN]
    o_ref, oka_ref, ova_ref, okb_ref, ovb_ref = refs[-5:]

    slope = slope_ref[...]
    ht = ht_ref[...]
    nh = N_HEADS
    qa, kna, vna, qb = ht[:, 0:nh], ht[:, nh:2 * nh], ht[:, 2 * nh:3 * nh], ht[:, 3 * nh:4 * nh]
    knb, vnb = ht[:, 4 * nh:4 * nh + KV_B], ht[:, 4 * nh + KV_B:4 * nh + 2 * KV_B]
    o_ref[:, 0:nh] = _decode(qa, lambda g: ka_ref[g], lambda g: va_ref[g], kna, vna,
                             list(range(N_HEADS)), slope, _dilated_multiplicity(WIN_A), float(len(DILATIONS)), None)
    o_ref[:, nh:2 * nh] = _decode(qb, lambda g: kb_ref[g], lambda g: vb_ref[g], knb, vnb,
                                  [h // G_B for h in range(N_HEADS)], slope, jnp.ones((1, WIN_B), F32), 1.0,
                                  sink_ref[...])

    def shift(h, carry):
        hot = lax.broadcasted_iota(I32, (1, N_HEADS), 1) == h
        oka_ref[h] = _shift_in(ka_ref[h], jnp.sum(jnp.where(hot, kna, 0.0), axis=1, keepdims=True))
        ova_ref[h] = _shift_in(va_ref[h], jnp.sum(jnp.where(hot, vna, 0.0), axis=1, keepdims=True))
        return carry
    lax.fori_loop(0, N_HEADS, shift, 0)
    for g in range(KV_B):
        okb_ref[g] = _shift_in(kb_ref[g], knb[:, g:g + 1])
        ovb_ref[g] = _shift_in(vb_ref[g], vnb[:, g:g + 1])


def _sample_attn(layer, slope, sink_l, ht, caches, prev_outs):
    nb = ht.shape[0]
    assert caches[0].shape[-1] == WIN_A and caches[2].shape[-1] == WIN_B
    col = lambda hh: pl.BlockSpec((None, HEAD_DIM, hh), lambda i: (i, 0, 0))
    cache = lambda hh, n: pl.BlockSpec((None, None, hh, HEAD_DIM, n), lambda i: (layer, i, 0, 0, 0))
    small = pl.BlockSpec((N_HEADS, 1), lambda i: (0, 0))
    in_specs = [small, small, col(HT_COLS),
                cache(N_HEADS, WIN_A), cache(N_HEADS, WIN_A), cache(KV_B, WIN_B), cache(KV_B, WIN_B)]
    args = [slope, sink_l, ht, *caches]
    assert len(args) == N_SAMPLE_IN
    aliases = {}
    if prev_outs is not None:
        in_specs += [pl.BlockSpec(memory_space=pl.ANY)] * 4
        args += list(prev_outs)
        aliases = {N_SAMPLE_IN + k: 1 + k for k in range(4)}
    return pl.pallas_call(
        _sample_body,
        grid=(nb,),
        in_specs=in_specs,
        out_specs=[col(2 * N_HEADS), cache(N_HEADS, WIN_A), cache(N_HEADS, WIN_A),
                   cache(KV_B, WIN_B), cache(KV_B, WIN_B)],
        out_shape=[jax.ShapeDtypeStruct((nb, HEAD_DIM, 2 * N_HEADS), F32)] + [
            jax.ShapeDtypeStruct(c.shape, F32) for c in caches],
        input_output_aliases=aliases,
        compiler_params=_arb(1),
        name="sample_attn",
    )(*args)


P_TR = 512
P_SBK = 256
P_TM = 512


def kernel(x_prompt, x_sample, cache_a_k, cache_a_v, cache_b_k, cache_b_v, c_prompt, c_sample, w_ada, b_ada, w_in, sinks_b, gain_a, gain_b, w_out, ln1_g, ln1_b, ln2_g, ln2_b, router_w, router_b, w_gate, w_up, w_down):
    nbatch, seq, _ = x_prompt.shape
    nsamp = x_sample.shape[0]
    t_p = nbatch * seq
    slopes = jnp.asarray(2.0 ** (-8.0 * np.arange(1, N_HEADS + 1) / N_HEADS), dtype=F32)
    slopes_col = slopes.reshape(N_HEADS, 1)

    perm = np.array([g * EPG + j for j in range(EPG) for g in range(N_GROUPS)])
    rw_t = router_w.T[perm]
    rb = router_b[perm].reshape(N_EXPERTS, 1)

    pad = (-(nbatch + nsamp)) % SUBLANES
    c_all = jnp.concatenate([c_prompt, c_sample, jnp.zeros((pad, D_MODEL), F32)], axis=0)
    mod_all = _ada(c_all, w_ada, b_ada)

    to_lanes = lambda c: jnp.transpose(c, (0, 1, 3, 4, 2))
    from_lanes = lambda c: jnp.transpose(c, (0, 1, 4, 2, 3))
    caches = [to_lanes(c) for c in (cache_a_k, cache_a_v, cache_b_k, cache_b_v)]
    new_caches = None

    xp = x_prompt.reshape(t_p, D_MODEL)
    xs = x_sample.reshape(nsamp, D_MODEL)
    pak, pav, pbk, pbv = [], [], [], []
    kb0 = 4 * A_WIDTH
    for l in range(DEPTH):
        mod_p = mod_all[l, :nbatch].reshape(nbatch * 6, 1, D_MODEL)
        mod_s = mod_all[l, nbatch:nbatch + nsamp].reshape(nsamp, 6, D_MODEL).transpose(1, 0, 2)
        ga, gb = gain_a[l].reshape(1, A_WIDTH), gain_b[l].reshape(1, A_WIDTH)
        l1g, l1b = ln1_g[l].reshape(1, D_MODEL), ln1_b[l].reshape(1, D_MODEL)
        l2g, l2b = ln2_g[l].reshape(1, D_MODEL), ln2_b[l].reshape(1, D_MODEL)

        h = _proj(xp, mod_p, w_in[l], P_TR, seq // P_TR)
        oa = _attn_a(h, slopes, nbatch, seq)
        ob = _attn_b(h, slopes, sinks_b[l], nbatch, seq)
        x1, u2, tail, lp, cnt = _merge(oa, ob, xp, mod_p, ga, gb, w_out[l], l1g, l1b, rw_t, rb, P_SBK,
                                       seq // P_SBK, P_SBK)
        xp = _moe_layer(u2, tail, lp, cnt, x1, mod_p, w_gate, w_up, w_down, l2g, l2b, l, P_SBK, seq // P_SBK, P_TM)
        h3 = h.reshape(nbatch, seq, IN_COLS)
        pak.append(h3[:, seq - WIN_A:, A_WIDTH:2 * A_WIDTH].reshape(nbatch, WIN_A, N_HEADS, HEAD_DIM))
        pav.append(h3[:, seq - WIN_A:, 2 * A_WIDTH:3 * A_WIDTH].reshape(nbatch, WIN_A, N_HEADS, HEAD_DIM))
        pbk.append(h3[:, seq - WIN_B:, kb0:kb0 + LANES].reshape(nbatch, WIN_B, KV_B, HEAD_DIM))
        pbv.append(h3[:, seq - WIN_B:, kb0 + LANES:kb0 + 2 * LANES].reshape(nbatch, WIN_B, KV_B, HEAD_DIM))

        hs = _proj(xs, mod_s, w_in[l], nsamp, 1)
        ht = jnp.transpose(hs.reshape(nsamp, HT_COLS, HEAD_DIM), (0, 2, 1))
        os_t, *new_caches = _sample_attn(l, slopes_col, sinks_b[l].reshape(N_HEADS, 1), ht, caches, new_caches)
        os = jnp.transpose(os_t, (0, 2, 1)).reshape(nsamp, 2 * A_WIDTH)
        oas, obs = os[:, :A_WIDTH], os[:, A_WIDTH:]
        x1s, u2s, tails, lps, cnts = _merge(oas, obs, xs, mod_s, ga, gb, w_out[l], l1g, l1b, rw_t, rb, nsamp, 1,
                                            nsamp)
        xs = _moe_layer(u2s, tails, lps, cnts, x1s, mod_s, w_gate, w_up, w_down, l2g, l2b, l, nsamp, 1, nsamp)

    sak, sav, sbk, sbv = [from_lanes(c) for c in new_caches]
    return (xp.reshape(nbatch, seq, D_MODEL), xs.reshape(nsamp, 1, D_MODEL),
            jnp.stack(pak), jnp.stack(pav), jnp.stack(pbk), jnp.stack(pbv), sak, sav, sbk, sbv)
```
